```python
import math
import jax, jax.numpy as jnp
from jax import lax
import numpy as np

D_MODEL = 1024
BATCH = 4
SEQ = 8192
DEPTH = 2
DEC_BATCH = 32
DEC_SEQ = 4
PAST_LEN = 16384
PAGE_SIZE = 128

N_MIXERS = 2
N_HEADS = 16
HEAD_DIM = D_MODEL // N_HEADS
CONV_WIDTH = 3
D_FF = 4 * D_MODEL
Q_BLOCK = 128
EPS = 1e-6
FORGET_BIAS = 4.0

kernel_name = "hybrid_shortconv_fox_decode_step"


def rmsnorm(x, g):
    xf = x.astype(jnp.float32)
    r = lax.rsqrt(jnp.mean(xf * xf, axis=-1, keepdims=True) + EPS)
    return (xf * r).astype(x.dtype) * g


def squared_relu_mlp(x, w1, w2):
    h = jax.nn.relu(x @ w1)
    return (h * h) @ w2


def short_conv_mixer(x, state, w_in, w_conv, w_out):
    T = x.shape[1]
    b, c, v = jnp.split(x @ w_in, 3, axis=-1)
    u = c * v
    up = jnp.concatenate([state.astype(u.dtype), u], axis=1)
    y = w_conv[0] * up[:, 0:T] + w_conv[1] * up[:, 1:T + 1] + w_conv[2] * up[:, 2:T + 2]
    out = (b * y) @ w_out
    return out, up[:, -(CONV_WIDTH - 1):]


def fox_project(x, w_qkv, w_f, b_f):
    N, T, _ = x.shape
    q, k, v = jnp.split(x @ w_qkv, 3, axis=-1)
    q = q.reshape(N, T, N_HEADS, HEAD_DIM)
    k = k.reshape(N, T, N_HEADS, HEAD_DIM)
    v = v.reshape(N, T, N_HEADS, HEAD_DIM)
    logf = jax.nn.log_sigmoid((x @ w_f + b_f).astype(jnp.float32))
    return q, k, v, logf


def fox_prompt_attention(q, k, v, logf):
    B, S, H, Dh = q.shape
    scale = 1.0 / math.sqrt(Dh)
    F = jnp.cumsum(logf, axis=1).transpose(0, 2, 1)
    nb = S // Q_BLOCK
    q_blocks = q.reshape(B, nb, Q_BLOCK, H, Dh).transpose(1, 0, 2, 3, 4)
    F_blocks = F.reshape(B, H, nb, Q_BLOCK).transpose(2, 0, 1, 3)
    kpos = jnp.arange(S)

    def block(args):
        qb, Fb, i = args
        s = jnp.einsum('bqhd,bkhd->bhqk', qb, k).astype(jnp.float32) * scale
        s = s + Fb[..., :, None] - F[:, :, None, :]
        qpos = i * Q_BLOCK + jnp.arange(Q_BLOCK)
        s = jnp.where(kpos[None, :] <= qpos[:, None], s, -jnp.inf)
        p = jax.nn.softmax(s, axis=-1).astype(v.dtype)
        return jnp.einsum('bhqk,bkhd->bqhd', p, v)

    o = lax.map(block, (q_blocks, F_blocks, jnp.arange(nb)))
    return o.transpose(1, 0, 2, 3, 4).reshape(B, S, H * Dh)


def fox_sample_attention(q, k, v, logf, ck, cv, clogf, page_table):
    N, T, H, Dh = q.shape
    scale = 1.0 / math.sqrt(Dh)
    kp = ck[page_table].reshape(N, -1, H, Dh)
    vp = cv[page_table].reshape(N, -1, H, Dh)
    lp = clogf[page_table].reshape(N, -1, H).astype(jnp.float32)
    P = kp.shape[1]
    suffix = (lax.cumsum(lp, axis=1, reverse=True) - lp).transpose(0, 2, 1)
    Fn = jnp.cumsum(logf, axis=1).transpose(0, 2, 1)
    s_past = jnp.einsum('bqhd,bkhd->bhqk', q, kp.astype(q.dtype)).astype(jnp.float32) * scale
    s_past = s_past + Fn[..., :, None] + suffix[..., None, :]
    s_new = jnp.einsum('bqhd,bkhd->bhqk', q, k).astype(jnp.float32) * scale
    s_new = s_new + Fn[..., :, None] - Fn[..., None, :]
    causal = jnp.arange(T)[None, :] <= jnp.arange(T)[:, None]
    s_new = jnp.where(causal, s_new, -jnp.inf)
    p = jax.nn.softmax(jnp.concatenate([s_past, s_new], axis=-1), axis=-1).astype(v.dtype)
    o = jnp.einsum('bhqk,bkhd->bqhd', p[..., :P], vp.astype(v.dtype)) + jnp.einsum('bhqk,bkhd->bqhd', p[..., P:], v)
    return o.reshape(N, T, H * Dh)


def setup_inputs(seed: int = 0) -> dict:
    key = jax.random.key(seed)
    ks = jax.random.split(key, 24)
    n_conv = (DEPTH + 1) // 2
    n_attn = DEPTH // 2
    n_pages = PAST_LEN // PAGE_SIZE
    n_used = DEC_BATCH * n_pages
    n_pool = (n_used * 5 + 3) // 4
    f32 = jnp.float32
    nrm = lambda k, shape, s: jax.random.normal(k, shape, f32) * s
    page_table = jax.random.permutation(ks[0], n_pool)[:n_used].reshape(DEC_BATCH, n_pages).astype(jnp.int32)
    return {
        "x_prompt": nrm(ks[1], (BATCH, SEQ, D_MODEL), 1.0),
        "x_sample": nrm(ks[2], (DEC_BATCH, DEC_SEQ, D_MODEL), 1.0),
        "state_conv": nrm(ks[3], (n_conv, DEC_BATCH, CONV_WIDTH - 1, D_MODEL), 1.0),
        "cache_k": nrm(ks[4], (n_attn, n_pool, PAGE_SIZE, N_HEADS, HEAD_DIM), 1.0),
        "cache_v": nrm(ks[5], (n_attn, n_pool, PAGE_SIZE, N_HEADS, HEAD_DIM), 1.0),
        "cache_logf": jax.nn.log_sigmoid(FORGET_BIAS + nrm(ks[6], (n_attn, n_pool, PAGE_SIZE, N_HEADS), 1.0)),
        "page_table": page_table,
        "conv_w_in": nrm(ks[7], (n_conv, D_MODEL, 3 * D_MODEL), D_MODEL ** -0.5),
        "conv_w": nrm(ks[8], (n_conv, CONV_WIDTH, D_MODEL), CONV_WIDTH ** -0.5),
        "conv_w_out": nrm(ks[9], (n_conv, D_MODEL, D_MODEL), D_MODEL ** -0.5),
        "attn_w_qkv": nrm(ks[10], (n_attn, D_MODEL, 3 * D_MODEL), D_MODEL ** -0.5),
        "attn_w_f": nrm(ks[11], (n_attn, D_MODEL, N_HEADS), D_MODEL ** -0.5),
        "attn_b_f": FORGET_BIAS + nrm(ks[12], (n_attn, N_HEADS), 0.1),
        "attn_w_o": nrm(ks[13], (n_attn, D_MODEL, D_MODEL), D_MODEL ** -0.5),
        "mlp_w1": nrm(ks[14], (DEPTH, D_MODEL, D_FF), D_MODEL ** -0.5),
        "mlp_w2": nrm(ks[15], (DEPTH, D_FF, D_MODEL), D_FF ** -0.5),
        "norm_pre_mix": 1.0 + nrm(ks[16], (DEPTH, D_MODEL), 0.02),
        "norm_post_mix": 1.0 + nrm(ks[17], (DEPTH, D_MODEL), 0.02),
        "norm_pre_mlp": 1.0 + nrm(ks[18], (DEPTH, D_MODEL), 0.02),
        "norm_post_mlp": 1.0 + nrm(ks[19], (DEPTH, D_MODEL), 0.02),
    }


def reference(x_prompt, x_sample, state_conv, cache_k, cache_v, cache_logf, page_table,
              conv_w_in, conv_w, conv_w_out, attn_w_qkv, attn_w_f, attn_b_f, attn_w_o,
              mlp_w1, mlp_w2, norm_pre_mix, norm_post_mix, norm_pre_mlp, norm_post_mlp):
    hp, hs = x_prompt, x_sample
    conv_p, conv_s = [], []
    kp_l, vp_l, lp_l, ks_l, vs_l, ls_l = [], [], [], [], [], []
    for i in range(DEPTH):
        j = i // N_MIXERS
        ap = rmsnorm(hp, norm_pre_mix[i])
        as_ = rmsnorm(hs, norm_pre_mix[i])
        if i % N_MIXERS == 0:
            zero_state = jnp.zeros((ap.shape[0], CONV_WIDTH - 1, D_MODEL), ap.dtype)
            mp, st_p = short_conv_mixer(ap, zero_state, conv_w_in[j], conv_w[j], conv_w_out[j])
            ms, st_s = short_conv_mixer(as_, state_conv[j], conv_w_in[j], conv_w[j], conv_w_out[j])
            conv_p.append(st_p)
            conv_s.append(st_s)
        else:
            q, k, v, lf = fox_project(ap, attn_w_qkv[j], attn_w_f[j], attn_b_f[j])
            mp = fox_prompt_attention(q, k, v, lf) @ attn_w_o[j]
            kp_l.append(k); vp_l.append(v); lp_l.append(lf)
            q2, k2, v2, lf2 = fox_project(as_, attn_w_qkv[j], attn_w_f[j], attn_b_f[j])
            ms = fox_sample_attention(q2, k2, v2, lf2, cache_k[j], cache_v[j], cache_logf[j], page_table) @ attn_w_o[j]
            ks_l.append(k2); vs_l.append(v2); ls_l.append(lf2)
        hp = hp + rmsnorm(mp, norm_post_mix[i])
        hs = hs + rmsnorm(ms, norm_post_mix[i])
        hp = hp + rmsnorm(squared_relu_mlp(rmsnorm(hp, norm_pre_mlp[i]), mlp_w1[i], mlp_w2[i]), norm_post_mlp[i])
        hs = hs + rmsnorm(squared_relu_mlp(rmsnorm(hs, norm_pre_mlp[i]), mlp_w1[i], mlp_w2[i]), norm_post_mlp[i])
    conv_state_prompt = jnp.stack(conv_p)
    conv_state_sample = jnp.stack(conv_s)
    k_prompt = jnp.stack(kp_l)
    v_prompt = jnp.stack(vp_l)
    logf_prompt = jnp.stack(lp_l)
    k_sample = jnp.stack(ks_l)
    v_sample = jnp.stack(vs_l)
    logf_sample = jnp.stack(ls_l)
    return (hp, hs, conv_state_prompt, conv_state_sample, k_prompt, v_prompt, logf_prompt, k_sample, v_sample, logf_sample)
```

```python
import functools

import jax
import jax.numpy as jnp
from jax import lax
from jax.experimental import pallas as pl
from jax.experimental.pallas import tpu as pltpu

D = 1024
H = 16
DH = 64
DFF = 4096
CONV_TAPS = 3
EPS = 1e-6
SCALE = 0.125
PAGE = 128

F32 = jnp.float32
BF16 = jnp.bfloat16
NT = (((1,), (1,)), ((), ()))

VMEM_LIMIT_BYTES = 56 * 1024 * 1024
SUBLANES = 8
LANES = 128


def _params(*sem):
    return pltpu.CompilerParams(dimension_semantics=sem, vmem_limit_bytes=VMEM_LIMIT_BYTES)


def _const_spec(shape):
    nd = len(shape)
    return pl.BlockSpec(shape, lambda *_: (0,) * nd)


def _rms(x, g):
    r = lax.rsqrt(jnp.mean(x * x, axis=-1, keepdims=True) + EPS)
    return x * r * g


def _log_sigmoid(z):
    return jnp.minimum(z, 0.0) - jnp.log1p(jnp.exp(-jnp.abs(z)))


def _split3(x):
    hi = x.astype(BF16)
    r = x - hi.astype(F32)
    mid = r.astype(BF16)
    lo = (r - mid.astype(F32)).astype(BF16)
    return hi, mid, lo


def _dot(a, b):
    return jnp.dot(a, b, preferred_element_type=F32)


def _dot_nt(a, b):
    return lax.dot_general(a, b, NT, preferred_element_type=F32)


def _conv_mixer_prompt_kernel(x_ref, win_ref, cw_ref, wout_ref, gpre_ref, gpost_ref,
                              h_ref, st_ref, ubuf, *, tm, nt):
    i = pl.program_id(1)

    @pl.when(i == 0)
    def _():
        ubuf[0:SUBLANES, :] = jnp.zeros((SUBLANES, D), F32)

    x = x_ref[...]
    a = _rms(x, gpre_ref[...]).astype(BF16)
    bcv = _dot(a, win_ref[...])
    u = bcv[:, D:2 * D] * bcv[:, 2 * D:]
    ubuf[SUBLANES:SUBLANES + tm, :] = u
    cw = cw_ref[...]
    y = (cw[0:1, :] * ubuf[SUBLANES - 2:SUBLANES - 2 + tm, :]
         + cw[1:2, :] * ubuf[SUBLANES - 1:SUBLANES - 1 + tm, :]
         + cw[2:3, :] * u)
    out = _dot((bcv[:, :D] * y).astype(BF16), wout_ref[...])
    h_ref[...] = x + _rms(out, gpost_ref[...])
    ubuf[0:SUBLANES, :] = ubuf[tm:tm + SUBLANES, :]

    @pl.when(i == nt - 1)
    def _():
        st_ref[0] = ubuf[SUBLANES - 2:SUBLANES, :]


def _conv_mixer_prompt(x, w_in, cw, w_out, g_pre, g_post, batch, seq, tm=512):
    nt = seq // tm
    return pl.pallas_call(
        functools.partial(_conv_mixer_prompt_kernel, tm=tm, nt=nt),
        grid=(batch, nt),
        in_specs=[
            pl.BlockSpec((tm, D), lambda b, i: (b * nt + i, 0)),
            _const_spec((D, 3 * D)),
            _const_spec((CONV_TAPS, D)),
            _const_spec((D, D)),
            _const_spec((1, D)),
            _const_spec((1, D)),
        ],
        out_specs=[
            pl.BlockSpec((tm, D), lambda b, i: (b * nt + i, 0)),
            pl.BlockSpec((1, CONV_TAPS - 1, D), lambda b, i: (b, 0, 0)),
        ],
        out_shape=[
            jax.ShapeDtypeStruct((batch * seq, D), F32),
            jax.ShapeDtypeStruct((batch, CONV_TAPS - 1, D), F32),
        ],
        scratch_shapes=[pltpu.VMEM((tm + SUBLANES, D), F32)],
        compiler_params=_params("arbitrary", "arbitrary"),
        name="conv_mixer_prompt",
    )(x, w_in, cw, w_out, g_pre, g_post)


def _conv_mixer_sample_kernel(x_ref, st0_ref, st1_ref, win_ref, cw_ref, wout_ref, gpre_ref,
                              gpost_ref, h_ref, st_ref, *, n, t):
    x = x_ref[...]
    a = _rms(x, gpre_ref[...]).astype(BF16)
    bcv = _dot(a, win_ref[...])
    u = bcv[:, D:2 * D] * bcv[:, 2 * D:]
    slabs = [st0_ref[...], st1_ref[...]] + [u[s * n:(s + 1) * n, :] for s in range(t)]
    cw = cw_ref[...]
    y = jnp.concatenate(
        [cw[0:1, :] * slabs[s] + cw[1:2, :] * slabs[s + 1] + cw[2:3, :] * slabs[s + 2]
         for s in range(t)], axis=0)
    out = _dot((bcv[:, :D] * y).astype(BF16), wout_ref[...])
    h_ref[...] = x + _rms(out, gpost_ref[...])
    st_ref[...] = jnp.concatenate(slabs[t:], axis=0)


def _conv_mixer_sample(x, st0, st1, w_in, cw, w_out, g_pre, g_post, n, t):
    rows = n * t
    shapes = [(rows, D), (n, D), (n, D), (D, 3 * D), (CONV_TAPS, D), (D, D), (1, D), (1, D)]
    return pl.pallas_call(
        functools.partial(_conv_mixer_sample_kernel, n=n, t=t),
        grid=(1,),
        in_specs=[_const_spec(s) for s in shapes],
        out_specs=[_const_spec((rows, D)), _const_spec(((CONV_TAPS - 1) * n, D))],
        out_shape=[
            jax.ShapeDtypeStruct((rows, D), F32),
            jax.ShapeDtypeStruct(((CONV_TAPS - 1) * n, D), F32),
        ],
        compiler_params=_params("arbitrary"),
        name="conv_mixer_sample",
    )(x, st0, st1, w_in, cw, w_out, g_pre, g_post)


def _mlp_kernel(h_ref, w1_ref, w2_ref, gpre_ref, gpost_ref, o_ref, *, n_chunks):
    h = h_ref[...]
    a = _rms(h, gpre_ref[...]).astype(BF16)
    ck = DFF // n_chunks
    acc = None
    for c in range(n_chunks):
        hm = jnp.maximum(_dot(a, w1_ref[:, c * ck:(c + 1) * ck]), 0.0)
        part = _dot((hm * hm).astype(BF16), w2_ref[c * ck:(c + 1) * ck, :])
        acc = part if acc is None else acc + part
    o_ref[...] = h + _rms(acc, gpost_ref[...])


def _mlp(h, w1, w2, g_pre, g_post, tm, name):
    rows = h.shape[0]
    return pl.pallas_call(
        functools.partial(_mlp_kernel, n_chunks=4),
        grid=(rows // tm,),
        in_specs=[
            pl.BlockSpec((tm, D), lambda i: (i, 0)),
            _const_spec((D, DFF)),
            _const_spec((DFF, D)),
            _const_spec((1, D)),
            _const_spec((1, D)),
        ],
        out_specs=pl.BlockSpec((tm, D), lambda i: (i, 0)),
        out_shape=jax.ShapeDtypeStruct((rows, D), F32),
        compiler_params=_params("arbitrary"),
        name=name,
    )(h, w1, w2, g_pre, g_post)


def _qkv_kernel(h_ref, wqkv_ref, wft_ref, bft_ref, gpre_ref,
                q_ref, k_ref, v_ref, kb_ref, vb_ref, lft_ref, ft_ref, carry, *, tm, seq):
    i = pl.program_id(0)
    a = _rms(h_ref[...], gpre_ref[...]).astype(BF16)
    qkv = _dot(a, wqkv_ref[...])
    q_ref[...] = (qkv[:, :D] * SCALE).astype(BF16)
    k = qkv[:, D:2 * D]
    v = qkv[:, 2 * D:]
    k_ref[...] = k
    v_ref[...] = v
    kb_ref[...] = k.astype(BF16)
    vb_ref[...] = v.astype(BF16)

    lft = _log_sigmoid(_dot_nt(wft_ref[...], a) + bft_ref[...])
    lft_ref[...] = lft

    r = lax.broadcasted_iota(jnp.int32, (tm, tm), 0)
    c = lax.broadcasted_iota(jnp.int32, (tm, tm), 1)
    keep = r <= c
    if seq < tm:
        shift = seq.bit_length() - 1
        keep = keep & ((r >> shift) == (c >> shift))
    tri = jnp.where(keep, 1.0, 0.0).astype(BF16)
    hi, mid, lo = _split3(lft)
    f = _dot(hi, tri) + _dot(mid, tri) + _dot(lo, tri)
    if seq > tm:
        @pl.when(i % (seq // tm) == 0)
        def _():
            carry[...] = jnp.zeros((H, 1), F32)
        f = f + carry[...]
        carry[...] = f[:, tm - 1:tm]
    ft_ref[...] = f


def _qkv(h, w_qkv, w_ft, b_ft, g_pre, tm, seq, name):
    rows = h.shape[0]
    assert seq & (seq - 1) == 0 and (seq % tm == 0 or tm % seq == 0)
    row_spec = pl.BlockSpec((tm, D), lambda i: (i, 0))
    lane_spec = pl.BlockSpec((H, tm), lambda i: (0, i))
    return pl.pallas_call(
        functools.partial(_qkv_kernel, tm=tm, seq=seq),
        grid=(rows // tm,),
        in_specs=[row_spec, _const_spec((D, 3 * D)), _const_spec((H, D)), _const_spec((H, 1)),
                  _const_spec((1, D))],
        out_specs=[row_spec, row_spec, row_spec, row_spec, row_spec, lane_spec, lane_spec],
        out_shape=[
            jax.ShapeDtypeStruct((rows, D), BF16),
            jax.ShapeDtypeStruct((rows, D), F32),
            jax.ShapeDtypeStruct((rows, D), F32),
            jax.ShapeDtypeStruct((rows, D), BF16),
            jax.ShapeDtypeStruct((rows, D), BF16),
            jax.ShapeDtypeStruct((H, rows), F32),
            jax.ShapeDtypeStruct((H, rows), F32),
        ],
        scratch_shapes=[pltpu.VMEM((H, 1), F32)],
        compiler_params=_params("arbitrary"),
        name=name,
    )(h, w_qkv, w_ft, b_ft, g_pre)


def _prompt_attn_kernel(q_ref, k_ref, v_ref, ft_ref, o_ref, m_ref, l_ref, acc_ref, *, tq):
    qi = pl.program_id(2)
    q = q_ref[...]
    lane = lax.broadcasted_iota(jnp.int32, (tq, 2 * DH), 1)
    zero = jnp.zeros_like(q)
    q_heads = (jnp.where(lane < DH, q, zero), jnp.where(lane >= DH, q, zero))
    m_ref[...] = jnp.full(m_ref.shape, -jnp.inf, F32)
    l_ref[...] = jnp.zeros(l_ref.shape, F32)
    acc_ref[...] = jnp.zeros(acc_ref.shape, F32)

    def step(off, diagonal):
        k = k_ref[pl.ds(off, tq), :]
        v = v_ref[pl.ds(off, tq), :]
        f = ft_ref[0, :, pl.ds(off, tq)]
        for hh in range(2):
            s = _dot_nt(q_heads[hh], k) - f[hh:hh + 1, :]
            if diagonal:
                row = lax.broadcasted_iota(jnp.int32, (tq, tq), 0)
                col = lax.broadcasted_iota(jnp.int32, (tq, tq), 1)
                s = jnp.where(col <= row, s, -jnp.inf)
            m_prev = m_ref[hh]
            m_new = jnp.maximum(m_prev, jnp.max(s, axis=1, keepdims=True))
            alpha = jnp.exp(m_prev - m_new)
            p = jnp.exp(s - m_new)
            l_ref[hh] = alpha * l_ref[hh] + jnp.sum(p, axis=1, keepdims=True)
            acc_ref[hh] = alpha * acc_ref[hh] + _dot(p.astype(BF16), v)
            m_ref[hh] = m_new

    def body(ki, carry):
        step(pl.multiple_of(ki * tq, tq), False)
        return carry

    lax.fori_loop(0, qi, body, 0)
    step(pl.multiple_of(qi * tq, tq), True)
    o_a = acc_ref[0] / l_ref[0]
    o_b = acc_ref[1] / l_ref[1]
    o_ref[...] = jnp.where(lane < DH, o_a, o_b).astype(BF16)


def _prompt_attn(q, kb, vb, ft, batch, seq, tq=512):
    nq = seq // tq
    hp = H // 2
    ft3 = ft.reshape(hp, 2, batch * seq)
    return pl.pallas_call(
        functools.partial(_prompt_attn_kernel, tq=tq),
        grid=(batch, hp, nq),
        in_specs=[
            pl.BlockSpec((tq, 2 * DH), lambda b, h, i: (b * nq + i, h)),
            pl.BlockSpec((seq, 2 * DH), lambda b, h, i: (b, h)),
            pl.BlockSpec((seq, 2 * DH), lambda b, h, i: (b, h)),
            pl.BlockSpec((1, 2, seq), lambda b, h, i: (h, 0, b)),
        ],
        out_specs=pl.BlockSpec((tq, 2 * DH), lambda b, h, i: (b * nq + i, h)),
        out_shape=jax.ShapeDtypeStruct((batch * seq, D), BF16),
        scratch_shapes=[
            pltpu.VMEM((2, tq, 1), F32),
            pltpu.VMEM((2, tq, 1), F32),
            pltpu.VMEM((2, tq, 2 * DH), F32),
        ],
        compiler_params=_params("arbitrary", "arbitrary", "arbitrary"),
        name="prompt_attn",
    )(q, kb, vb, ft3)


def _proj_res_kernel(o_ref, w_ref, g_ref, h_ref, out_ref):
    m = _dot(o_ref[...].astype(BF16), w_ref[...])
    out_ref[...] = h_ref[...] + _rms(m, g_ref[...])


def _proj_res(o, w, g, h, tm, name):
    rows = h.shape[0]
    row_spec = pl.BlockSpec((tm, D), lambda i: (i, 0))
    return pl.pallas_call(
        _proj_res_kernel,
        grid=(rows // tm,),
        in_specs=[row_spec, _const_spec((D, D)), _const_spec((1, D)), row_spec],
        out_specs=row_spec,
        out_shape=jax.ShapeDtypeStruct((rows, D), F32),
        compiler_params=_params("arbitrary"),
        name=name,
    )(o, w, g, h)


def _sample_attn_kernel(pt_ref, q_ref, k2_ref, v2_ref, fnt_ref, ck_ref, cv_ref, cl_ref, o_ref,
                        qbd, m_ref, l_ref, acc_ref, carry, *, n_pages, t):
    del pt_ref
    p = pl.program_id(1)
    rows = t * H
    row = lax.broadcasted_iota(jnp.int32, (rows, D), 0)
    lane = lax.broadcasted_iota(jnp.int32, (rows, D), 1)
    own_head = (lane >> 6) == (row & (H - 1))

    @pl.when(p == 0)
    def _():
        q = q_ref[0]
        rep = jnp.concatenate([jnp.broadcast_to(q[s:s + 1, :], (H, D)) for s in range(t)], axis=0)
        qbd[...] = jnp.where(own_head, rep, 0.0).astype(BF16)
        m_ref[...] = jnp.full(m_ref.shape, -jnp.inf, F32)
        l_ref[...] = jnp.zeros(l_ref.shape, F32)
        acc_ref[...] = jnp.zeros(acc_ref.shape, F32)
        carry[...] = jnp.zeros(carry.shape, F32)

    def update(s, vb):
        m_prev = m_ref[...]
        m_new = jnp.maximum(m_prev, jnp.max(s, axis=1, keepdims=True))
        alpha = jnp.exp(m_prev - m_new)
        pr = jnp.exp(s - m_new)
        l_ref[...] = alpha * l_ref[...] + jnp.sum(pr, axis=1, keepdims=True)
        acc_ref[...] = alpha * acc_ref[...] + _dot(pr.astype(BF16), vb)
        m_ref[...] = m_new

    s = _dot_nt(qbd[...], ck_ref[0].astype(BF16))
    lpt = cl_ref[0].T
    j = lax.broadcasted_iota(jnp.int32, (PAGE, PAGE), 0)
    kk = lax.broadcasted_iota(jnp.int32, (PAGE, PAGE), 1)
    later = jnp.where(j > kk, 1.0, 0.0).astype(BF16)
    hi, mid, lo = _split3(lpt)
    bias = _dot(hi, later) + _dot(mid, later) + _dot(lo, later) + carry[...]
    s = s + jnp.concatenate([bias] * t, axis=0)
    update(s, cv_ref[0].astype(BF16))
    carry[...] = carry[...] + jnp.sum(lpt, axis=1, keepdims=True)

    @pl.when(p == n_pages - 1)
    def _():
        pad = jnp.zeros((PAGE - SUBLANES, D), F32)
        k2 = jnp.concatenate([k2_ref[0], pad], axis=0).astype(BF16)
        v2 = jnp.concatenate([v2_ref[0], pad], axis=0).astype(BF16)
        s2 = _dot_nt(qbd[...], k2) - jnp.concatenate([fnt_ref[0]] * t, axis=0)
        r2 = lax.broadcasted_iota(jnp.int32, (rows, PAGE), 0)
        c2 = lax.broadcasted_iota(jnp.int32, (rows, PAGE), 1)
        s2 = jnp.where(c2 <= (r2 >> 4), s2, -jnp.inf)
        update(s2, v2)
        o = jnp.where(own_head, acc_ref[...] / l_ref[...], 0.0)
        o_ref[0] = jnp.sum(o.reshape(t, H, D), axis=1)


def _sample_attn(page_table, q8, k8, v8, fnt, cache_k, cache_v, cache_logf, t):
    n, n_pages = page_table.shape
    rows = t * H
    tok_spec = pl.BlockSpec((1, SUBLANES, D), lambda b, p, pt: (b, 0, 0))
    page = lambda b, p, pt: (pt[b, n_pages - 1 - p], 0, 0)
    grid_spec = pltpu.PrefetchScalarGridSpec(
        num_scalar_prefetch=1,
        grid=(n, n_pages),
        in_specs=[
            tok_spec, tok_spec, tok_spec,
            pl.BlockSpec((1, H, PAGE), lambda b, p, pt: (b, 0, 0)),
            pl.BlockSpec((1, PAGE, D), page),
            pl.BlockSpec((1, PAGE, D), page),
            pl.BlockSpec((1, PAGE, H), page),
        ],
        out_specs=pl.BlockSpec((1, t, D), lambda b, p, pt: (b, 0, 0)),
        scratch_shapes=[
            pltpu.VMEM((rows, D), BF16),
            pltpu.VMEM((rows, 1), F32),
            pltpu.VMEM((rows, 1), F32),
            pltpu.VMEM((rows, D), F32),
            pltpu.VMEM((H, 1), F32),
        ],
    )
    return pl.pallas_call(
        functools.partial(_sample_attn_kernel, n_pages=n_pages, t=t),
        grid_spec=grid_spec,
        out_shape=jax.ShapeDtypeStruct((n, t, D), F32),
        compiler_params=_params("arbitrary", "arbitrary"),
        name="sample_attn",
    )(page_table, q8, k8, v8, fnt, cache_k, cache_v, cache_logf)


def kernel(x_prompt, x_sample, state_conv, cache_k, cache_v, cache_logf, page_table, conv_w_in, conv_w, conv_w_out, attn_w_qkv, attn_w_f, attn_b_f, attn_w_o, mlp_w1, mlp_w2, norm_pre_mix, norm_post_mix, norm_pre_mlp, norm_post_mlp):
    batch, seq, _ = x_prompt.shape
    n, t, _ = x_sample.shape
    n_pool = cache_k.shape[1]
    tm_p, tm_s = 512, n * t
    gain = lambda g, i: g[i].reshape(1, D)

    w_in, w_out = conv_w_in[0].astype(BF16), conv_w_out[0].astype(BF16)
    w1, w2 = mlp_w1[0].astype(BF16), mlp_w2[0].astype(BF16)
    hp, conv_p = _conv_mixer_prompt(x_prompt.reshape(batch * seq, D), w_in, conv_w[0], w_out,
                                    gain(norm_pre_mix, 0), gain(norm_post_mix, 0), batch, seq, tm_p)
    xs = x_sample.transpose(1, 0, 2).reshape(t * n, D)
    hs, conv_s = _conv_mixer_sample(xs, state_conv[0, :, 0], state_conv[0, :, 1], w_in, conv_w[0], w_out,
                                    gain(norm_pre_mix, 0), gain(norm_post_mix, 0), n, t)
    hp = _mlp(hp, w1, w2, gain(norm_pre_mlp, 0), gain(norm_post_mlp, 0), tm_p, "mlp0_prompt")
    hs = _mlp(hs, w1, w2, gain(norm_pre_mlp, 0), gain(norm_post_mlp, 0), tm_s, "mlp0_sample")
    hs = hs.reshape(t, n, D).transpose(1, 0, 2).reshape(n * t, D)

    w_qkv, w_o = attn_w_qkv[0].astype(BF16), attn_w_o[0].astype(BF16)
    w_ft, b_ft = attn_w_f[0].T.astype(BF16), attn_b_f[0].reshape(H, 1)
    w1, w2 = mlp_w1[1].astype(BF16), mlp_w2[1].astype(BF16)

    q, k_p, v_p, kb, vb, lft_p, ft_p = _qkv(hp, w_qkv, w_ft, b_ft, gain(norm_pre_mix, 1), tm_p, seq, "qkv_prompt")
    o_p = _prompt_attn(q, kb, vb, ft_p, batch, seq)
    hp = _proj_res(o_p, w_o, gain(norm_post_mix, 1), hp, tm_p, "attn_out_prompt")
    hp = _mlp(hp, w1, w2, gain(norm_pre_mlp, 1), gain(norm_post_mlp, 1), tm_p, "mlp1_prompt")

    q_s, k_s, v_s, _, _, lft_s, ft_s = _qkv(hs, w_qkv, w_ft, b_ft, gain(norm_pre_mix, 1), tm_s, t, "qkv_sample")
    pad_rows = lambda a: jnp.pad(a.reshape(n, t, D), ((0, 0), (0, SUBLANES - t), (0, 0)))
    fnt = jnp.pad(ft_s.reshape(H, n, t).transpose(1, 0, 2), ((0, 0), (0, 0), (0, PAGE - t)))
    o_s = _sample_attn(page_table, pad_rows(q_s.astype(F32)), pad_rows(k_s), pad_rows(v_s), fnt,
                       cache_k[0].reshape(n_pool, PAGE, D), cache_v[0].reshape(n_pool, PAGE, D),
                       cache_logf[0], t)
    hs = _proj_res(o_s.reshape(n * t, D), w_o, gain(norm_post_mix, 1), hs, tm_s, "attn_out_sample")
    hs = _mlp(hs, w1, w2, gain(norm_pre_mlp, 1), gain(norm_post_mlp, 1), tm_s, "mlp1_sample")

    heads = lambda a, b_, s_: a.reshape(1, b_, s_, H, DH)
    gates = lambda a, b_, s_: a.T.reshape(1, b_, s_, H)
    return (hp.reshape(batch, seq, D), hs.reshape(n, t, D),
            conv_p[None], conv_s.reshape(CONV_TAPS - 1, n, D).transpose(1, 0, 2)[None],
            heads(k_p, batch, seq), heads(v_p, batch, seq), gates(lft_p, batch, seq),
            heads(k_s, n, t), heads(v_s, n, t), gates(lft_s, n, t))
```

```python
import functools

import jax
import jax.numpy as jnp
from jax import lax
from jax.experimental import pallas as pl
from jax.experimental.pallas import tpu as pltpu

D = 1024
H = 16
DH = 64
DFF = 4096
CONV_TAPS = 3
EPS = 1e-6
SCALE = 0.125
PAGE = 128

F32 = jnp.float32
BF16 = jnp.bfloat16
NT = (((1,), (1,)), ((), ()))

VMEM_LIMIT_BYTES = 56 * 1024 * 1024
SUBLANES = 8
LANES = 128


def _params(*sem):
    return pltpu.CompilerParams(dimension_semantics=sem, vmem_limit_bytes=VMEM_LIMIT_BYTES)


def _const_spec(shape):
    nd = len(shape)
    return pl.BlockSpec(shape, lambda *_: (0,) * nd)


def _rms(x, g):
    r = lax.rsqrt(jnp.mean(x * x, axis=-1, keepdims=True) + EPS)
    return x * r * g


def _log_sigmoid(z):
    return jnp.minimum(z, 0.0) - jnp.log1p(jnp.exp(-jnp.abs(z)))


def _split3(x):
    hi = x.astype(BF16)
    r = x - hi.astype(F32)
    mid = r.astype(BF16)
    lo = (r - mid.astype(F32)).astype(BF16)
    return hi, mid, lo


def _dot(a, b):
    return jnp.dot(a, b, preferred_element_type=F32)


def _dot_nt(a, b):
    return lax.dot_general(a, b, NT, preferred_element_type=F32)


def _conv_mixer_prompt_kernel(x_ref, win_ref, cw_ref, wout_ref, gpre_ref, gpost_ref,
                              h_ref, st_ref, ubuf, *, tm, nt):
    i = pl.program_id(1)

    @pl.when(i == 0)
    def _():
        ubuf[0:SUBLANES, :] = jnp.zeros((SUBLANES, D), F32)

    x = x_ref[...]
    a = _rms(x, gpre_ref[...]).astype(BF16)
    bcv = _dot(a, win_ref[...])
    u = bcv[:, D:2 * D] * bcv[:, 2 * D:]
    ubuf[SUBLANES:SUBLANES + tm, :] = u
    cw = cw_ref[...]
    y = (cw[0:1, :] * ubuf[SUBLANES - 2:SUBLANES - 2 + tm, :]
         + cw[1:2, :] * ubuf[SUBLANES - 1:SUBLANES - 1 + tm, :]
         + cw[2:3, :] * u)
    out = _dot((bcv[:, :D] * y).astype(BF16), wout_ref[...])
    h_ref[...] = x + _rms(out, gpost_ref[...])
    ubuf[0:SUBLANES, :] = ubuf[tm:tm + SUBLANES, :]

    @pl.when(i == nt - 1)
    def _():
        st_ref[0] = ubuf[SUBLANES - 2:SUBLANES, :]


def _conv_mixer_prompt(x, w_in, cw, w_out, g_pre, g_post, batch, seq, tm=512):
    nt = seq // tm
    return pl.pallas_call(
        functools.partial(_conv_mixer_prompt_kernel, tm=tm, nt=nt),
        grid=(batch, nt),
        in_specs=[
            pl.BlockSpec((tm, D), lambda b, i: (b * nt + i, 0)),
            _const_spec((D, 3 * D)),
            _const_spec((CONV_TAPS, D)),
            _const_spec((D, D)),
            _const_spec((1, D)),
            _const_spec((1, D)),
        ],
        out_specs=[
            pl.BlockSpec((tm, D), lambda b, i: (b * nt + i, 0)),
            pl.BlockSpec((1, CONV_TAPS - 1, D), lambda b, i: (b, 0, 0)),
        ],
        out_shape=[
            jax.ShapeDtypeStruct((batch * seq, D), F32),
            jax.ShapeDtypeStruct((batch, CONV_TAPS - 1, D), F32),
        ],
        scratch_shapes=[pltpu.VMEM((tm + SUBLANES, D), F32)],
        compiler_params=_params("arbitrary", "arbitrary"),
        name="conv_mixer_prompt",
    )(x, w_in, cw, w_out, g_pre, g_post)


def _conv_mixer_sample_kernel(x_ref, st0_ref, st1_ref, win_ref, cw_ref, wout_ref, gpre_ref,
                              gpost_ref, h_ref, st_ref, *, n, t):
    x = x_ref[...]
    a = _rms(x, gpre_ref[...]).astype(BF16)
    bcv = _dot(a, win_ref[...])
    u = bcv[:, D:2 * D] * bcv[:, 2 * D:]
    slabs = [st0_ref[...], st1_ref[...]] + [u[s * n:(s + 1) * n, :] for s in range(t)]
    cw = cw_ref[...]
    y = jnp.concatenate(
        [cw[0:1, :] * slabs[s] + cw[1:2, :] * slabs[s + 1] + cw[2:3, :] * slabs[s + 2]
         for s in range(t)], axis=0)
    out = _dot((bcv[:, :D] * y).astype(BF16), wout_ref[...])
    h_ref[...] = x + _rms(out, gpost_ref[...])
    st_ref[...] = jnp.concatenate(slabs[t:], axis=0)


def _conv_mixer_sample(x, st0, st1, w_in, cw, w_out, g_pre, g_post, n, t):
    rows = n * t
    shapes = [(rows, D), (n, D), (n, D), (D, 3 * D), (CONV_TAPS, D), (D, D), (1, D), (1, D)]
    return pl.pallas_call(
        functools.partial(_conv_mixer_sample_kernel, n=n, t=t),
        grid=(1,),
        in_specs=[_const_spec(s) for s in shapes],
        out_specs=[_const_spec((rows, D)), _const_spec(((CONV_TAPS - 1) * n, D))],
        out_shape=[
            jax.ShapeDtypeStruct((rows, D), F32),
            jax.ShapeDtypeStruct(((CONV_TAPS - 1) * n, D), F32),
        ],
        compiler_params=_params("arbitrary"),
        name="conv_mixer_sample",
    )(x, st0, st1, w_in, cw, w_out, g_pre, g_post)


def _mlp_kernel(h_ref, w1_ref, w2_ref, gpre_ref, gpost_ref, o_ref, *, n_chunks):
    h = h_ref[...]
    a = _rms(h, gpre_ref[...]).astype(BF16)
    ck = DFF // n_chunks
    acc = None
    for c in range(n_chunks):
        hm = jnp.maximum(_dot(a, w1_ref[:, c * ck:(c + 1) * ck]), 0.0)
        part = _dot((hm * hm).astype(BF16), w2_ref[c * ck:(c + 1) * ck, :])
        acc = part if acc is None else acc + part
    o_ref[...] = h + _rms(acc, gpost_ref[...])


def _mlp(h, w1, w2, g_pre, g_post, tm, name):
    rows = h.shape[0]
    return pl.pallas_call(
        functools.partial(_mlp_kernel, n_chunks=4),
        grid=(rows // tm,),
        in_specs=[
            pl.BlockSpec((tm, D), lambda i: (i, 0)),
            _const_spec((D, DFF)),
            _const_spec((DFF, D)),
            _const_spec((1, D)),
            _const_spec((1, D)),
        ],
        out_specs=pl.BlockSpec((tm, D), lambda i: (i, 0)),
        out_shape=jax.ShapeDtypeStruct((rows, D), F32),
        compiler_params=_params("arbitrary"),
        name=name,
    )(h, w1, w2, g_pre, g_post)


def _qkv_kernel(h_ref, wqkvt_ref, wft_ref, bft_ref, gpre_ref,
                qt_ref, kt_ref, vt_ref, krm_ref, vtb_ref, lft_ref, ft_ref, carry, *, tm, seq):
    i = pl.program_id(1)
    a = _rms(h_ref[...], gpre_ref[...]).astype(BF16)
    qkvt = _dot_nt(wqkvt_ref[...], a)
    qt_ref[0] = (qkvt[:D, :] * SCALE).astype(BF16)
    kt = qkvt[D:2 * D, :]
    vt = qkvt[2 * D:, :]
    kt_ref[0] = kt
    vt_ref[0] = vt
    vtb_ref[0] = vt.astype(BF16)
    krm_ref[...] = kt.T.astype(BF16)

    lft = _log_sigmoid(_dot_nt(wft_ref[...], a) + bft_ref[...])
    lft_ref[0] = lft

    r = lax.broadcasted_iota(jnp.int32, (tm, tm), 0)
    c = lax.broadcasted_iota(jnp.int32, (tm, tm), 1)
    keep = r <= c
    if seq < tm:
        shift = seq.bit_length() - 1
        keep = keep & ((r >> shift) == (c >> shift))
    tri = jnp.where(keep, 1.0, 0.0).astype(BF16)
    hi, mid, lo = _split3(lft)
    f = _dot(hi, tri) + _dot(mid, tri) + _dot(lo, tri)
    if seq > tm:
        @pl.when(i == 0)
        def _():
            carry[...] = jnp.zeros((H, 1), F32)
        f = f + carry[...]
        carry[...] = f[:, tm - 1:tm]
    ft_ref[0] = f


def _qkv(h, w_qkvt, w_ft, b_ft, g_pre, batch, rows_per_batch, tm, seq, name):
    assert seq & (seq - 1) == 0 and (seq == rows_per_batch if seq > tm else tm % seq == 0)
    nt = rows_per_batch // tm
    row_spec = pl.BlockSpec((tm, D), lambda b, i: (b * nt + i, 0))
    feat_spec = pl.BlockSpec((1, D, tm), lambda b, i: (b, 0, i))
    gate_spec = pl.BlockSpec((1, H, tm), lambda b, i: (b, 0, i))
    feat = lambda dt: jax.ShapeDtypeStruct((batch, D, rows_per_batch), dt)
    gate = jax.ShapeDtypeStruct((batch, H, rows_per_batch), F32)
    return pl.pallas_call(
        functools.partial(_qkv_kernel, tm=tm, seq=seq),
        grid=(batch, nt),
        in_specs=[row_spec, _const_spec((3 * D, D)), _const_spec((H, D)), _const_spec((H, 1)),
                  _const_spec((1, D))],
        out_specs=[feat_spec, feat_spec, feat_spec, row_spec, feat_spec, gate_spec, gate_spec],
        out_shape=[
            feat(BF16),
            feat(F32),
            feat(F32),
            jax.ShapeDtypeStruct((batch * rows_per_batch, D), BF16),
            feat(BF16),
            gate,
            gate,
        ],
        scratch_shapes=[pltpu.VMEM((H, 1), F32)],
        compiler_params=_params("arbitrary", "arbitrary"),
        name=name,
    )(h, w_qkvt, w_ft, b_ft, g_pre)


def _prompt_attn_kernel(qt_ref, k_ref, vt_ref, ft_ref, o_ref, fb_ref, acc_ref, *, tq, seq):
    qi = pl.program_id(2)

    @pl.when(qi == 0)
    def _():
        def fill(c, carry):
            off = pl.multiple_of(c * tq, tq)
            for hh in range(2):
                row = ft_ref[0, 0, hh:hh + 1, pl.ds(off, tq)]
                fb_ref[hh, pl.ds(off, tq), :] = jnp.broadcast_to(row, (LANES, tq)).T
            return carry
        lax.fori_loop(0, seq // tq, fill, 0)

    qt = qt_ref[0]
    sub = lax.broadcasted_iota(jnp.int32, (2 * DH, tq), 0)
    zero = jnp.zeros_like(qt)
    qt_heads = (jnp.where(sub < DH, qt, zero), jnp.where(sub >= DH, qt, zero))
    acc_ref[...] = jnp.zeros(acc_ref.shape, F32)
    neg = jnp.full((1, tq), -jnp.inf, F32)
    nil = jnp.zeros((1, tq), F32)

    def step(off, stats, diagonal):
        k = k_ref[pl.ds(off, tq), :]
        vt = vt_ref[0, :, pl.ds(off, tq)]
        out = []
        for hh in range(2):
            m_prev, l_prev = stats[hh]
            st = _dot(k, qt_heads[hh])
            st = st - pltpu.repeat(fb_ref[hh, pl.ds(off, tq), :], tq // LANES, axis=1)
            if diagonal:
                key = lax.broadcasted_iota(jnp.int32, (tq, tq), 0)
                qry = lax.broadcasted_iota(jnp.int32, (tq, tq), 1)
                st = jnp.where(key <= qry, st, -jnp.inf)
            m_new = jnp.maximum(m_prev, jnp.max(st, axis=0, keepdims=True))
            alpha = jnp.exp(m_prev - m_new)
            p = jnp.exp(st - m_new)
            l_new = alpha * l_prev + jnp.sum(p, axis=0, keepdims=True)
            pv = _dot(vt[hh * DH:(hh + 1) * DH, :], p.astype(BF16))
            acc_ref[hh] = alpha * acc_ref[hh] + pv
            out.append((m_new, l_new))
        return tuple(out)

    def body(ki, stats):
        return step(pl.multiple_of(ki * tq, tq), stats, False)

    stats = lax.fori_loop(0, qi, body, ((neg, nil), (neg, nil)))
    stats = step(pl.multiple_of(qi * tq, tq), stats, True)
    ot = jnp.concatenate([acc_ref[0] / stats[0][1], acc_ref[1] / stats[1][1]], axis=0)
    o_ref[...] = ot.T.astype(BF16)


def _prompt_attn(qt, k_rm, vtb, ft, batch, seq, tq=512):
    nq = seq // tq
    hp = H // 2
    ft4 = ft.reshape(batch, hp, 2, seq)
    return pl.pallas_call(
        functools.partial(_prompt_attn_kernel, tq=tq, seq=seq),
        grid=(batch, hp, nq),
        in_specs=[
            pl.BlockSpec((1, 2 * DH, tq), lambda b, h, i: (b, h, i)),
            pl.BlockSpec((seq, 2 * DH), lambda b, h, i: (b, h)),
            pl.BlockSpec((1, 2 * DH, seq), lambda b, h, i: (b, h, 0)),
            pl.BlockSpec((1, 1, 2, seq), lambda b, h, i: (b, h, 0, 0)),
        ],
        out_specs=pl.BlockSpec((tq, 2 * DH), lambda b, h, i: (b * nq + i, h)),
        out_shape=jax.ShapeDtypeStruct((batch * seq, D), BF16),
        scratch_shapes=[
            pltpu.VMEM((2, seq, LANES), F32),
            pltpu.VMEM((2, DH, tq), F32),
        ],
        compiler_params=_params("arbitrary", "arbitrary", "arbitrary"),
        name="prompt_attn",
    )(qt, k_rm, vtb, ft4)


def _proj_res_kernel(o_ref, w_ref, g_ref, h_ref, out_ref):
    m = _dot(o_ref[...].astype(BF16), w_ref[...])
    out_ref[...] = h_ref[...] + _rms(m, g_ref[...])


def _proj_res(o, w, g, h, tm, name):
    rows = h.shape[0]
    row_spec = pl.BlockSpec((tm, D), lambda i: (i, 0))
    return pl.pallas_call(
        _proj_res_kernel,
        grid=(rows // tm,),
        in_specs=[row_spec, _const_spec((D, D)), _const_spec((1, D)), row_spec],
        out_specs=row_spec,
        out_shape=jax.ShapeDtypeStruct((rows, D), F32),
        compiler_params=_params("arbitrary"),
        name=name,
    )(o, w, g, h)


def _sample_attn_kernel(pt_ref, q_ref, k2_ref, v2_ref, fnt_ref, *refs, n_steps, pg, t):
    del pt_ref
    cl_refs, ck_refs, cv_refs = refs[:pg], refs[pg:2 * pg], refs[2 * pg:3 * pg]
    o_ref, qbd, m_ref, l_ref, acc_ref, carry = refs[3 * pg:]
    g = pl.program_id(1)
    rows = t * H
    row = lax.broadcasted_iota(jnp.int32, (rows, D), 0)
    lane = lax.broadcasted_iota(jnp.int32, (rows, D), 1)
    own_head = (lane >> 6) == (row & (H - 1))

    @pl.when(g == 0)
    def _():
        q = q_ref[0]
        rep = jnp.concatenate([jnp.broadcast_to(q[s:s + 1, :], (H, D)) for s in range(t)], axis=0)
        qbd[...] = jnp.where(own_head, rep, 0.0).astype(BF16)
        m_ref[...] = jnp.full(m_ref.shape, -jnp.inf, F32)
        l_ref[...] = jnp.zeros(l_ref.shape, F32)
        acc_ref[...] = jnp.zeros(acc_ref.shape, F32)
        carry[...] = jnp.zeros(carry.shape, F32)

    def update(s, pv_fn):
        m_prev = m_ref[...]
        m_new = jnp.maximum(m_prev, jnp.max(s, axis=1, keepdims=True))
        alpha = jnp.exp(m_prev - m_new)
        pr = jnp.exp(s - m_new).astype(BF16)
        l_ref[...] = alpha * l_ref[...] + jnp.sum(pr.astype(F32), axis=1, keepdims=True)
        acc_ref[...] = alpha * acc_ref[...] + pv_fn(pr)
        m_ref[...] = m_new

    qb = qbd[...]
    lpt = jnp.concatenate([r[...] for r in cl_refs], axis=0)
    j = lax.broadcasted_iota(jnp.int32, (PAGE, PAGE), 0)
    kk = lax.broadcasted_iota(jnp.int32, (PAGE, PAGE), 1)
    later = jnp.where(j > kk, 1.0, 0.0).astype(BF16)
    hi, mid, lo = _split3(lpt)
    within = _dot(hi, later) + _dot(mid, later) + _dot(lo, later)
    after = carry[...]
    scores = []
    for i in range(pg):
        bias = within[i * H:(i + 1) * H, :] + after
        s = _dot(qb, ck_refs[i][...].reshape(D, PAGE).astype(BF16))
        scores.append(s + jnp.concatenate([bias] * t, axis=0))
        after = after + jnp.sum(lpt[i * H:(i + 1) * H, :], axis=1, keepdims=True)
    carry[...] = after

    def cache_pv(pr):
        out = None
        for i in range(pg):
            part = _dot_nt(pr[:, i * PAGE:(i + 1) * PAGE], cv_refs[i][...].reshape(D, PAGE).astype(BF16))
            out = part if out is None else out + part
        return out

    update(jnp.concatenate(scores, axis=1), cache_pv)

    @pl.when(g == n_steps - 1)
    def _():
        pad = jnp.zeros((PAGE - SUBLANES, D), F32)
        k2 = jnp.concatenate([k2_ref[0], pad], axis=0).astype(BF16)
        v2 = jnp.concatenate([v2_ref[0], pad], axis=0).astype(BF16)
        s2 = _dot_nt(qb, k2) - jnp.concatenate([fnt_ref[0]] * t, axis=0)
        r2 = lax.broadcasted_iota(jnp.int32, (rows, PAGE), 0)
        c2 = lax.broadcasted_iota(jnp.int32, (rows, PAGE), 1)
        s2 = jnp.where(c2 <= (r2 >> 4), s2, -jnp.inf)
        update(s2, lambda pr: _dot(pr, v2))
        o = jnp.where(own_head, acc_ref[...] / l_ref[...], 0.0)
        o_ref[0] = jnp.sum(o.reshape(t, H, D), axis=1)


def _sample_attn(page_table, q8, k8, v8, fnt, cache_kt, cache_vt, cache_lt, t, pg=8):
    n, n_pages = page_table.shape
    n_steps = n_pages // pg
    rows = t * H
    tok_spec = pl.BlockSpec((1, SUBLANES, D), lambda b, g, pt: (b, 0, 0))

    def page(i, nd):
        return lambda b, g, pt: (0, pt[b, n_pages - 1 - (g * pg + i)]) + (0,) * nd

    grid_spec = pltpu.PrefetchScalarGridSpec(
        num_scalar_prefetch=1,
        grid=(n, n_steps),
        in_specs=(
            [tok_spec, tok_spec, tok_spec, pl.BlockSpec((1, H, PAGE), lambda b, g, pt: (b, 0, 0))]
            + [pl.BlockSpec((None, None, H, PAGE), page(i, 2)) for i in range(pg)]
            + [pl.BlockSpec((None, None, H, DH, PAGE), page(i, 3)) for i in range(pg)]
            + [pl.BlockSpec((None, None, H, DH, PAGE), page(i, 3)) for i in range(pg)]
        ),
        out_specs=pl.BlockSpec((1, t, D), lambda b, g, pt: (b, 0, 0)),
        scratch_shapes=[
            pltpu.VMEM((rows, D), BF16),
            pltpu.VMEM((rows, 1), F32),
            pltpu.VMEM((rows, 1), F32),
            pltpu.VMEM((rows, D), F32),
            pltpu.VMEM((H, 1), F32),
        ],
    )
    return pl.pallas_call(
        functools.partial(_sample_attn_kernel, n_steps=n_steps, pg=pg, t=t),
        grid_spec=grid_spec,
        out_shape=jax.ShapeDtypeStruct((n, t, D), F32),
        compiler_params=_params("arbitrary", "arbitrary"),
        name="sample_attn",
    )(page_table, q8, k8, v8, fnt, *([cache_lt] * pg), *([cache_kt] * pg), *([cache_vt] * pg))


def kernel(x_prompt, x_sample, state_conv, cache_k, cache_v, cache_logf, page_table, conv_w_in, conv_w, conv_w_out, attn_w_qkv, attn_w_f, attn_b_f, attn_w_o, mlp_w1, mlp_w2, norm_pre_mix, norm_post_mix, norm_pre_mlp, norm_post_mlp):
    batch, seq, _ = x_prompt.shape
    n, t, _ = x_sample.shape
    tm_p, tm_s = 512, n * t
    gain = lambda g, i: g[i].reshape(1, D)

    w_in, w_out = conv_w_in[0].astype(BF16), conv_w_out[0].astype(BF16)
    w1, w2 = mlp_w1[0].astype(BF16), mlp_w2[0].astype(BF16)
    hp, conv_p = _conv_mixer_prompt(x_prompt.reshape(batch * seq, D), w_in, conv_w[0], w_out,
                                    gain(norm_pre_mix, 0), gain(norm_post_mix, 0), batch, seq, tm_p)
    xs = x_sample.transpose(1, 0, 2).reshape(t * n, D)
    hs, conv_s = _conv_mixer_sample(xs, state_conv[0, :, 0], state_conv[0, :, 1], w_in, conv_w[0], w_out,
                                    gain(norm_pre_mix, 0), gain(norm_post_mix, 0), n, t)
    hp = _mlp(hp, w1, w2, gain(norm_pre_mlp, 0), gain(norm_post_mlp, 0), tm_p, "mlp0_prompt")
    hs = _mlp(hs, w1, w2, gain(norm_pre_mlp, 0), gain(norm_post_mlp, 0), tm_s, "mlp0_sample")
    hs = hs.reshape(t, n, D).transpose(1, 0, 2).reshape(n * t, D)

    w_qkvt, w_o = attn_w_qkv[0].T.astype(BF16), attn_w_o[0].astype(BF16)
    w_ft, b_ft = attn_w_f[0].T.astype(BF16), attn_b_f[0].reshape(H, 1)
    w1, w2 = mlp_w1[1].astype(BF16), mlp_w2[1].astype(BF16)

    qt, kt_p, vt_p, k_rm, vtb, lft_p, ft_p = _qkv(hp, w_qkvt, w_ft, b_ft, gain(norm_pre_mix, 1),
                                                  batch, seq, tm_p, seq, "qkv_prompt")
    o_p = _prompt_attn(qt, k_rm, vtb, ft_p, batch, seq)
    hp = _proj_res(o_p, w_o, gain(norm_post_mix, 1), hp, tm_p, "attn_out_prompt")
    hp = _mlp(hp, w1, w2, gain(norm_pre_mlp, 1), gain(norm_post_mlp, 1), tm_p, "mlp1_prompt")

    qt_s, kt_s, vt_s, _, _, lft_s, ft_s = _qkv(hs, w_qkvt, w_ft, b_ft, gain(norm_pre_mix, 1),
                                               1, n * t, tm_s, t, "qkv_sample")
    k_s, v_s = kt_s[0].T, vt_s[0].T
    pad_rows = lambda a: jnp.pad(a.reshape(n, t, D), ((0, 0), (0, SUBLANES - t), (0, 0)))
    fnt = jnp.pad(ft_s[0].reshape(H, n, t).transpose(1, 0, 2), ((0, 0), (0, 0), (0, PAGE - t)))
    o_s = _sample_attn(page_table, pad_rows(qt_s[0].T.astype(F32)), pad_rows(k_s), pad_rows(v_s), fnt,
                       cache_k.transpose(0, 1, 3, 4, 2), cache_v.transpose(0, 1, 3, 4, 2),
                       cache_logf.transpose(0, 1, 3, 2), t)
    hs = _proj_res(o_s.reshape(n * t, D), w_o, gain(norm_post_mix, 1), hs, tm_s, "attn_out_sample")
    hs = _mlp(hs, w1, w2, gain(norm_pre_mlp, 1), gain(norm_post_mlp, 1), tm_s, "mlp1_sample")

    heads_t = lambda a: a.reshape(1, batch, H, DH, seq).transpose(0, 1, 4, 2, 3)
    return (hp.reshape(batch, seq, D), hs.reshape(n, t, D),
            conv_p[None], conv_s.reshape(CONV_TAPS - 1, n, D).transpose(1, 0, 2)[None],
            heads_t(kt_p), heads_t(vt_p), lft_p.transpose(0, 2, 1)[None],
            k_s.reshape(1, n, t, H, DH), v_s.reshape(1, n, t, H, DH), lft_s[0].T.reshape(1, n, t, H))
```

```python
import functools

import jax
import jax.numpy as jnp
from jax import lax
from jax.experimental import pallas as pl
from jax.experimental.pallas import tpu as pltpu

D = 1024
H = 16
DH = 64
DFF = 4096
CONV_TAPS = 3
EPS = 1e-6
SCALE = 0.125
PAGE = 128
LOG2E = 1.4426950408889634

F32 = jnp.float32
BF16 = jnp.bfloat16
NT = (((1,), (1,)), ((), ()))

VMEM_LIMIT_BYTES = 56 * 1024 * 1024
SUBLANES = 8
LANES = 128


def _params(*sem):
    return pltpu.CompilerParams(dimension_semantics=sem, vmem_limit_bytes=VMEM_LIMIT_BYTES)


def _const_spec(shape):
    nd = len(shape)
    return pl.BlockSpec(shape, lambda *_: (0,) * nd)


def _rms(x, g):
    r = lax.rsqrt(jnp.mean(x * x, axis=-1, keepdims=True) + EPS)
    return x * r * g


def _log_sigmoid(z):
    return jnp.minimum(z, 0.0) - jnp.log1p(jnp.exp(-jnp.abs(z)))


def _split3(x):
    hi = x.astype(BF16)
    r = x - hi.astype(F32)
    mid = r.astype(BF16)
    lo = (r - mid.astype(F32)).astype(BF16)
    return hi, mid, lo


def _dot(a, b):
    return jnp.dot(a, b, preferred_element_type=F32)


def _dot_nt(a, b):
    return lax.dot_general(a, b, NT, preferred_element_type=F32)


def _conv_mixer_prompt_kernel(x_ref, win_ref, cw_ref, wout_ref, gpre_ref, gpost_ref,
                              h_ref, st_ref, ubuf, *, tm, nt):
    i = pl.program_id(1)

    @pl.when(i == 0)
    def _():
        ubuf[0:SUBLANES, :] = jnp.zeros((SUBLANES, D), F32)

    x = x_ref[...]
    a = _rms(x, gpre_ref[...]).astype(BF16)
    bcv = _dot(a, win_ref[...])
    u = bcv[:, D:2 * D] * bcv[:, 2 * D:]
    ubuf[SUBLANES:SUBLANES + tm, :] = u
    cw = cw_ref[...]
    y = (cw[0:1, :] * ubuf[SUBLANES - 2:SUBLANES - 2 + tm, :]
         + cw[1:2, :] * ubuf[SUBLANES - 1:SUBLANES - 1 + tm, :]
         + cw[2:3, :] * u)
    out = _dot((bcv[:, :D] * y).astype(BF16), wout_ref[...])
    h_ref[...] = x + _rms(out, gpost_ref[...])
    ubuf[0:SUBLANES, :] = ubuf[tm:tm + SUBLANES, :]

    @pl.when(i == nt - 1)
    def _():
        st_ref[0] = ubuf[SUBLANES - 2:SUBLANES, :]


def _conv_mixer_prompt(x, w_in, cw, w_out, g_pre, g_post, batch, seq, tm=512):
    nt = seq // tm
    return pl.pallas_call(
        functools.partial(_conv_mixer_prompt_kernel, tm=tm, nt=nt),
        grid=(batch, nt),
        in_specs=[
            pl.BlockSpec((tm, D), lambda b, i: (b * nt + i, 0)),
            _const_spec((D, 3 * D)),
            _const_spec((CONV_TAPS, D)),
            _const_spec((D, D)),
            _const_spec((1, D)),
            _const_spec((1, D)),
        ],
        out_specs=[
            pl.BlockSpec((tm, D), lambda b, i: (b * nt + i, 0)),
            pl.BlockSpec((1, CONV_TAPS - 1, D), lambda b, i: (b, 0, 0)),
        ],
        out_shape=[
            jax.ShapeDtypeStruct((batch * seq, D), F32),
            jax.ShapeDtypeStruct((batch, CONV_TAPS - 1, D), F32),
        ],
        scratch_shapes=[pltpu.VMEM((tm + SUBLANES, D), F32)],
        compiler_params=_params("arbitrary", "arbitrary"),
        name="conv_mixer_prompt",
    )(x, w_in, cw, w_out, g_pre, g_post)


def _conv_mixer_sample_kernel(x_ref, st0_ref, st1_ref, win_ref, cw_ref, wout_ref, gpre_ref,
                              gpost_ref, h_ref, st_ref, *, n, t):
    x = x_ref[...]
    a = _rms(x, gpre_ref[...]).astype(BF16)
    bcv = _dot(a, win_ref[...])
    u = bcv[:, D:2 * D] * bcv[:, 2 * D:]
    slabs = [st0_ref[...], st1_ref[...]] + [u[s * n:(s + 1) * n, :] for s in range(t)]
    cw = cw_ref[...]
    y = jnp.concatenate(
        [cw[0:1, :] * slabs[s] + cw[1:2, :] * slabs[s + 1] + cw[2:3, :] * slabs[s + 2]
         for s in range(t)], axis=0)
    out = _dot((bcv[:, :D] * y).astype(BF16), wout_ref[...])
    h_ref[...] = x + _rms(out, gpost_ref[...])
    st_ref[...] = jnp.concatenate(slabs[t:], axis=0)


def _conv_mixer_sample(x, st0, st1, w_in, cw, w_out, g_pre, g_post, n, t):
    rows = n * t
    shapes = [(rows, D), (n, D), (n, D), (D, 3 * D), (CONV_TAPS, D), (D, D), (1, D), (1, D)]
    return pl.pallas_call(
        functools.partial(_conv_mixer_sample_kernel, n=n, t=t),
        grid=(1,),
        in_specs=[_const_spec(s) for s in shapes],
        out_specs=[_const_spec((rows, D)), _const_spec(((CONV_TAPS - 1) * n, D))],
        out_shape=[
            jax.ShapeDtypeStruct((rows, D), F32),
            jax.ShapeDtypeStruct(((CONV_TAPS - 1) * n, D), F32),
        ],
        compiler_params=_params("arbitrary"),
        name="conv_mixer_sample",
    )(x, st0, st1, w_in, cw, w_out, g_pre, g_post)


def _mlp_kernel(h_ref, w1_ref, w2_ref, gpre_ref, gpost_ref, o_ref, *, n_chunks):
    h = h_ref[...]
    a = _rms(h, gpre_ref[...]).astype(BF16)
    ck = DFF // n_chunks
    acc = None
    for c in range(n_chunks):
        hm = jnp.maximum(_dot(a, w1_ref[:, c * ck:(c + 1) * ck]), 0.0)
        part = _dot((hm * hm).astype(BF16), w2_ref[c * ck:(c + 1) * ck, :])
        acc = part if acc is None else acc + part
    o_ref[...] = h + _rms(acc, gpost_ref[...])


def _mlp(h, w1, w2, g_pre, g_post, tm, name):
    rows = h.shape[0]
    return pl.pallas_call(
        functools.partial(_mlp_kernel, n_chunks=4),
        grid=(rows // tm,),
        in_specs=[
            pl.BlockSpec((tm, D), lambda i: (i, 0)),
            _const_spec((D, DFF)),
            _const_spec((DFF, D)),
            _const_spec((1, D)),
            _const_spec((1, D)),
        ],
        out_specs=pl.BlockSpec((tm, D), lambda i: (i, 0)),
        out_shape=jax.ShapeDtypeStruct((rows, D), F32),
        compiler_params=_params("arbitrary"),
        name=name,
    )(h, w1, w2, g_pre, g_post)


def _qkv_kernel(h_ref, wqkvt_ref, wft_ref, bft_ref, gpre_ref,
                qt_ref, kt_ref, vt_ref, krm_ref, vtb_ref, lft_ref, ft_ref, carry, *, tm, seq, q_scale):
    i = pl.program_id(1)
    a = _rms(h_ref[...], gpre_ref[...]).astype(BF16)
    qkvt = _dot_nt(wqkvt_ref[...], a)
    qt_ref[0] = (qkvt[:D, :] * q_scale).astype(BF16)
    kt = qkvt[D:2 * D, :]
    vt = qkvt[2 * D:, :]
    kt_ref[0] = kt
    vt_ref[0] = vt
    vtb_ref[0] = vt.astype(BF16)
    krm_ref[...] = kt.T.astype(BF16)

    lft = _log_sigmoid(_dot_nt(wft_ref[...], a) + bft_ref[...])
    lft_ref[0] = lft

    r = lax.broadcasted_iota(jnp.int32, (tm, tm), 0)
    c = lax.broadcasted_iota(jnp.int32, (tm, tm), 1)
    keep = r <= c
    if seq < tm:
        shift = seq.bit_length() - 1
        keep = keep & ((r >> shift) == (c >> shift))
    tri = jnp.where(keep, 1.0, 0.0).astype(BF16)
    hi, mid, lo = _split3(lft)
    f = _dot(hi, tri) + _dot(mid, tri) + _dot(lo, tri)
    if seq > tm:
        @pl.when(i == 0)
        def _():
            carry[...] = jnp.zeros((H, 1), F32)
        f = f + carry[...]
        carry[...] = f[:, tm - 1:tm]
    ft_ref[0] = f


def _qkv(h, w_qkvt, w_ft, b_ft, g_pre, batch, rows_per_batch, tm, seq, q_scale, name):
    assert seq & (seq - 1) == 0 and (seq == rows_per_batch if seq > tm else tm % seq == 0)
    nt = rows_per_batch // tm
    row_spec = pl.BlockSpec((tm, D), lambda b, i: (b * nt + i, 0))
    feat_spec = pl.BlockSpec((1, D, tm), lambda b, i: (b, 0, i))
    gate_spec = pl.BlockSpec((1, H, tm), lambda b, i: (b, 0, i))
    feat = lambda dt: jax.ShapeDtypeStruct((batch, D, rows_per_batch), dt)
    gate = jax.ShapeDtypeStruct((batch, H, rows_per_batch), F32)
    return pl.pallas_call(
        functools.partial(_qkv_kernel, tm=tm, seq=seq, q_scale=q_scale),
        grid=(batch, nt),
        in_specs=[row_spec, _const_spec((3 * D, D)), _const_spec((H, D)), _const_spec((H, 1)),
                  _const_spec((1, D))],
        out_specs=[feat_spec, feat_spec, feat_spec, row_spec, feat_spec, gate_spec, gate_spec],
        out_shape=[
            feat(BF16),
            feat(F32),
            feat(F32),
            jax.ShapeDtypeStruct((batch * rows_per_batch, D), BF16),
            feat(BF16),
            gate,
            gate,
        ],
        scratch_shapes=[pltpu.VMEM((H, 1), F32)],
        compiler_params=_params("arbitrary", "arbitrary"),
        name=name,
    )(h, w_qkvt, w_ft, b_ft, g_pre)


def _prompt_attn_kernel(qt_ref, k_ref, vt_ref, ft_ref, o_ref, kaug_ref, acc_ref, s_ref, *, tq, seq):
    qi = pl.program_id(2)

    @pl.when(qi == 0)
    def _():
        lane = lax.broadcasted_iota(jnp.int32, (tq, LANES), 1)

        def fill(c, carry):
            off = pl.multiple_of(c * tq, tq)
            kaug_ref[pl.ds(off, tq), 0:LANES] = k_ref[pl.ds(off, tq), :]
            aug = jnp.zeros((tq, LANES), F32)
            for hh in range(2):
                row = ft_ref[0, 0, hh:hh + 1, pl.ds(off, tq)] * (-LOG2E)
                col = jnp.broadcast_to(row, (LANES, tq)).T
                hi = col.astype(BF16).astype(F32)
                r1 = col - hi
                mid = r1.astype(BF16).astype(F32)
                lo = r1 - mid
                aug = jnp.where(lane == 3 * hh, hi,
                                jnp.where(lane == 3 * hh + 1, mid, jnp.where(lane == 3 * hh + 2, lo, aug)))
            kaug_ref[pl.ds(off, tq), LANES:2 * LANES] = aug.astype(BF16)
            return carry
        lax.fori_loop(0, seq // tq, fill, 0)

    qt = qt_ref[0]
    sub = lax.broadcasted_iota(jnp.int32, (2 * DH, tq), 0)
    zero = jnp.zeros_like(qt)
    q_heads = (jnp.where(sub < DH, qt, zero), jnp.where(sub >= DH, qt, zero))
    q_aug = []
    for hh in range(2):
        sel = jnp.where((sub >= 3 * hh) & (sub < 3 * hh + 3), 1.0, 0.0).astype(BF16)
        q_aug.append(jnp.concatenate([q_heads[hh], sel], axis=0))
    ones_rows = jnp.ones((2 * SUBLANES, tq), BF16)
    acc_ref[...] = jnp.zeros(acc_ref.shape, F32)
    neg = jnp.full((1, tq), -jnp.inf, F32)

    def scores(off):
        k = kaug_ref[pl.ds(off, tq), :]
        return tuple(_dot(k, q_aug[hh]) for hh in range(2))

    def consume(off, sts, stats, diagonal):
        vt = vt_ref[0, :, pl.ds(off, tq)]
        out = []
        for hh in range(2):
            m_prev = stats[hh]
            st = sts[hh]
            if diagonal:
                key = lax.broadcasted_iota(jnp.int32, (tq, tq), 0)
                qry = lax.broadcasted_iota(jnp.int32, (tq, tq), 1)
                st = jnp.where(key <= qry, st, -jnp.inf)
            m_new = jnp.maximum(m_prev, jnp.max(st, axis=0, keepdims=True))
            alpha = jnp.exp2(m_prev - m_new)
            p = jnp.exp2(st - m_new).astype(BF16)
            va = jnp.concatenate([vt[hh * DH:(hh + 1) * DH, :], ones_rows], axis=0)
            acc_ref[hh] = alpha * acc_ref[hh] + _dot(va, p)
            out.append(m_new)
        return tuple(out)

    def body(ki, stats):
        sts = (s_ref[0], s_ref[1])
        nxt = scores(pl.multiple_of((ki + 1) * tq, tq))
        stats = consume(pl.multiple_of(ki * tq, tq), sts, stats, False)
        s_ref[0] = nxt[0]
        s_ref[1] = nxt[1]
        return stats

    first = scores(0)
    s_ref[0] = first[0]
    s_ref[1] = first[1]
    stats = lax.fori_loop(0, qi, body, (neg, neg))
    consume(pl.multiple_of(qi * tq, tq), (s_ref[0], s_ref[1]), stats, True)
    a0, a1 = acc_ref[0], acc_ref[1]
    ot = jnp.concatenate([a0[:DH] / a0[DH:DH + 1], a1[:DH] / a1[DH:DH + 1]], axis=0)
    o_ref[...] = ot.T.astype(BF16)


def _prompt_attn(qt, k_rm, vtb, ft, batch, seq, tq=512):
    nq = seq // tq
    hp = H // 2
    ft4 = ft.reshape(batch, hp, 2, seq)
    return pl.pallas_call(
        functools.partial(_prompt_attn_kernel, tq=tq, seq=seq),
        grid=(batch, hp, nq),
        in_specs=[
            pl.BlockSpec((1, 2 * DH, tq), lambda b, h, i: (b, h, i)),
            pl.BlockSpec((seq, 2 * DH), lambda b, h, i: (b, h)),
            pl.BlockSpec((1, 2 * DH, seq), lambda b, h, i: (b, h, 0)),
            pl.BlockSpec((1, 1, 2, seq), lambda b, h, i: (b, h, 0, 0)),
        ],
        out_specs=pl.BlockSpec((tq, 2 * DH), lambda b, h, i: (b * nq + i, h)),
        out_shape=jax.ShapeDtypeStruct((batch * seq, D), BF16),
        scratch_shapes=[
            pltpu.VMEM((seq, 4 * DH), BF16),
            pltpu.VMEM((2, DH + 2 * SUBLANES, tq), F32),
            pltpu.VMEM((2, tq, tq), F32),
        ],
        compiler_params=_params("arbitrary", "arbitrary", "arbitrary"),
        name="prompt_attn",
    )(qt, k_rm, vtb, ft4)


def _proj_res_kernel(o_ref, w_ref, g_ref, h_ref, out_ref):
    m = _dot(o_ref[...].astype(BF16), w_ref[...])
    out_ref[...] = h_ref[...] + _rms(m, g_ref[...])


def _proj_res(o, w, g, h, tm, name):
    rows = h.shape[0]
    row_spec = pl.BlockSpec((tm, D), lambda i: (i, 0))
    return pl.pallas_call(
        _proj_res_kernel,
        grid=(rows // tm,),
        in_specs=[row_spec, _const_spec((D, D)), _const_spec((1, D)), row_spec],
        out_specs=row_spec,
        out_shape=jax.ShapeDtypeStruct((rows, D), F32),
        compiler_params=_params("arbitrary"),
        name=name,
    )(o, w, g, h)


def _sample_attn_kernel(pt_ref, q_ref, k2_ref, v2_ref, fnt_ref, *refs, n_steps, pg, t):
    del pt_ref
    cl_refs, ck_refs, cv_refs = refs[:pg], refs[pg:2 * pg], refs[2 * pg:3 * pg]
    o_ref, qbd, m_ref, l_ref, acc_ref, carry = refs[3 * pg:]
    g = pl.program_id(1)
    rows = t * H
    row = lax.broadcasted_iota(jnp.int32, (rows, D), 0)
    lane = lax.broadcasted_iota(jnp.int32, (rows, D), 1)
    own_head = (lane >> 6) == (row & (H - 1))

    @pl.when(g == 0)
    def _():
        q = q_ref[0]
        rep = jnp.concatenate([jnp.broadcast_to(q[s:s + 1, :], (H, D)) for s in range(t)], axis=0)
        qbd[...] = jnp.where(own_head, rep, 0.0).astype(BF16)
        m_ref[...] = jnp.full(m_ref.shape, -jnp.inf, F32)
        l_ref[...] = jnp.zeros(l_ref.shape, F32)
        acc_ref[...] = jnp.zeros(acc_ref.shape, F32)
        carry[...] = jnp.zeros(carry.shape, F32)

    def update(s, pv_fn):
        m_prev = m_ref[...]
        m_new = jnp.maximum(m_prev, jnp.max(s, axis=1, keepdims=True))
        alpha = jnp.exp(m_prev - m_new)
        pr = jnp.exp(s - m_new).astype(BF16)
        l_ref[...] = alpha * l_ref[...] + jnp.sum(pr.astype(F32), axis=1, keepdims=True)
        acc_ref[...] = alpha * acc_ref[...] + pv_fn(pr)
        m_ref[...] = m_new

    qb = qbd[...]
    lpt = jnp.concatenate([r[...] for r in cl_refs], axis=0)
    j = lax.broadcasted_iota(jnp.int32, (PAGE, PAGE), 0)
    kk = lax.broadcasted_iota(jnp.int32, (PAGE, PAGE), 1)
    later = jnp.where(j > kk, 1.0, 0.0).astype(BF16)
    hi, mid, lo = _split3(lpt)
    within = _dot(hi, later) + _dot(mid, later) + _dot(lo, later)
    after = carry[...]
    scores = []
    for i in range(pg):
        bias = within[i * H:(i + 1) * H, :] + after
        s = _dot(qb, ck_refs[i][...].reshape(D, PAGE).astype(BF16))
        scores.append(s + jnp.concatenate([bias] * t, axis=0))
        after = after + jnp.sum(lpt[i * H:(i + 1) * H, :], axis=1, keepdims=True)
    carry[...] = after

    def cache_pv(pr):
        out = None
        for i in range(pg):
            part = _dot_nt(pr[:, i * PAGE:(i + 1) * PAGE], cv_refs[i][...].reshape(D, PAGE).astype(BF16))
            out = part if out is None else out + part
        return out

    update(jnp.concatenate(scores, axis=1), cache_pv)

    @pl.when(g == n_steps - 1)
    def _():
        pad = jnp.zeros((PAGE - SUBLANES, D), F32)
        k2 = jnp.concatenate([k2_ref[0], pad], axis=0).astype(BF16)
        v2 = jnp.concatenate([v2_ref[0], pad], axis=0).astype(BF16)
        s2 = _dot_nt(qb, k2) - jnp.concatenate([fnt_ref[0]] * t, axis=0)
        r2 = lax.broadcasted_iota(jnp.int32, (rows, PAGE), 0)
        c2 = lax.broadcasted_iota(jnp.int32, (rows, PAGE), 1)
        s2 = jnp.where(c2 <= (r2 >> 4), s2, -jnp.inf)
        update(s2, lambda pr: _dot(pr, v2))
        o = jnp.where(own_head, acc_ref[...] / l_ref[...], 0.0)
        o_ref[0] = jnp.sum(o.reshape(t, H, D), axis=1)


def _sample_attn(page_table, q8, k8, v8, fnt, cache_kt, cache_vt, cache_lt, t, pg=8):
    n, n_pages = page_table.shape
    n_steps = n_pages // pg
    rows = t * H
    tok_spec = pl.BlockSpec((1, SUBLANES, D), lambda b, g, pt: (b, 0, 0))

    def page(i, nd):
        return lambda b, g, pt: (0, pt[b, n_pages - 1 - (g * pg + i)]) + (0,) * nd

    grid_spec = pltpu.PrefetchScalarGridSpec(
        num_scalar_prefetch=1,
        grid=(n, n_steps),
        in_specs=(
            [tok_spec, tok_spec, tok_spec, pl.BlockSpec((1, H, PAGE), lambda b, g, pt: (b, 0, 0))]
            + [pl.BlockSpec((None, None, H, PAGE), page(i, 2)) for i in range(pg)]
            + [pl.BlockSpec((None, None, H, DH, PAGE), page(i, 3)) for i in range(pg)]
            + [pl.BlockSpec((None, None, H, DH, PAGE), page(i, 3)) for i in range(pg)]
        ),
        out_specs=pl.BlockSpec((1, t, D), lambda b, g, pt: (b, 0, 0)),
        scratch_shapes=[
            pltpu.VMEM((rows, D), BF16),
            pltpu.VMEM((rows, 1), F32),
            pltpu.VMEM((rows, 1), F32),
            pltpu.VMEM((rows, D), F32),
            pltpu.VMEM((H, 1), F32),
        ],
    )
    return pl.pallas_call(
        functools.partial(_sample_attn_kernel, n_steps=n_steps, pg=pg, t=t),
        grid_spec=grid_spec,
        out_shape=jax.ShapeDtypeStruct((n, t, D), F32),
        compiler_params=_params("arbitrary", "arbitrary"),
        name="sample_attn",
    )(page_table, q8, k8, v8, fnt, *([cache_lt] * pg), *([cache_kt] * pg), *([cache_vt] * pg))


def kernel(x_prompt, x_sample, state_conv, cache_k, cache_v, cache_logf, page_table, conv_w_in, conv_w, conv_w_out, attn_w_qkv, attn_w_f, attn_b_f, attn_w_o, mlp_w1, mlp_w2, norm_pre_mix, norm_post_mix, norm_pre_mlp, norm_post_mlp):
    batch, seq, _ = x_prompt.shape
    n, t, _ = x_sample.shape
    tm_p, tm_s = 512, n * t
    gain = lambda g, i: g[i].reshape(1, D)

    w_in, w_out = conv_w_in[0].astype(BF16), conv_w_out[0].astype(BF16)
    w1, w2 = mlp_w1[0].astype(BF16), mlp_w2[0].astype(BF16)
    hp, conv_p = _conv_mixer_prompt(x_prompt.reshape(batch * seq, D), w_in, conv_w[0], w_out,
                                    gain(norm_pre_mix, 0), gain(norm_post_mix, 0), batch, seq, tm_p)
    xs = x_sample.transpose(1, 0, 2).reshape(t * n, D)
    hs, conv_s = _conv_mixer_sample(xs, state_conv[0, :, 0], state_conv[0, :, 1], w_in, conv_w[0], w_out,
                                    gain(norm_pre_mix, 0), gain(norm_post_mix, 0), n, t)
    hp = _mlp(hp, w1, w2, gain(norm_pre_mlp, 0), gain(norm_post_mlp, 0), tm_p, "mlp0_prompt")
    hs = _mlp(hs, w1, w2, gain(norm_pre_mlp, 0), gain(norm_post_mlp, 0), tm_s, "mlp0_sample")
    hs = hs.reshape(t, n, D).transpose(1, 0, 2).reshape(n * t, D)

    w_qkvt, w_o = attn_w_qkv[0].T.astype(BF16), attn_w_o[0].astype(BF16)
    w_ft, b_ft = attn_w_f[0].T.astype(BF16), attn_b_f[0].reshape(H, 1)
    w1, w2 = mlp_w1[1].astype(BF16), mlp_w2[1].astype(BF16)

    qt, kt_p, vt_p, k_rm, vtb, lft_p, ft_p = _qkv(hp, w_qkvt, w_ft, b_ft, gain(norm_pre_mix, 1),
                                                  batch, seq, tm_p, seq, SCALE * LOG2E, "qkv_prompt")
    o_p = _prompt_attn(qt, k_rm, vtb, ft_p, batch, seq)
    hp = _proj_res(o_p, w_o, gain(norm_post_mix, 1), hp, tm_p, "attn_out_prompt")
    hp = _mlp(hp, w1, w2, gain(norm_pre_mlp, 1), gain(norm_post_mlp, 1), tm_p, "mlp1_prompt")

    qt_s, kt_s, vt_s, _, _, lft_s, ft_s = _qkv(hs, w_qkvt, w_ft, b_ft, gain(norm_pre_mix, 1),
                                               1, n * t, tm_s, t, SCALE, "qkv_sample")
    k_s, v_s = kt_s[0].T, vt_s[0].T
    pad_rows = lambda a: jnp.pad(a.reshape(n, t, D), ((0, 0), (0, SUBLANES - t), (0, 0)))
    fnt = jnp.pad(ft_s[0].reshape(H, n, t).transpose(1, 0, 2), ((0, 0), (0, 0), (0, PAGE - t)))
    o_s = _sample_attn(page_table, pad_rows(qt_s[0].T.astype(F32)), pad_rows(k_s), pad_rows(v_s), fnt,
                       cache_k.transpose(0, 1, 3, 4, 2), cache_v.transpose(0, 1, 3, 4, 2),
                       cache_logf.transpose(0, 1, 3, 2), t)
    hs = _proj_res(o_s.reshape(n * t, D), w_o, gain(norm_post_mix, 1), hs, tm_s, "attn_out_sample")
    hs = _mlp(hs, w1, w2, gain(norm_pre_mlp, 1), gain(norm_post_mlp, 1), tm_s, "mlp1_sample")

    heads_t = lambda a: a.reshape(1, batch, H, DH, seq).transpose(0, 1, 4, 2, 3)
    return (hp.reshape(batch, seq, D), hs.reshape(n, t, D),
            conv_p[None], conv_s.reshape(CONV_TAPS - 1, n, D).transpose(1, 0, 2)[None],
            heads_t(kt_p), heads_t(vt_p), lft_p.transpose(0, 2, 1)[None],
            k_s.reshape(1, n, t, H, DH), v_s.reshape(1, n, t, H, DH), lft_s[0].T.reshape(1, n, t, H))
```

```python
import functools

import jax
import jax.numpy as jnp
from jax import lax
from jax.experimental import pallas as pl
from jax.experimental.pallas import tpu as pltpu

D = 1024
H = 16
DH = 64
DFF = 4096
CONV_TAPS = 3
EPS = 1e-6
SCALE = 0.125
PAGE = 128
LOG2E = 1.4426950408889634

F32 = jnp.float32
BF16 = jnp.bfloat16
NT = (((1,), (1,)), ((), ()))

VMEM_LIMIT_BYTES = 56 * 1024 * 1024
SUBLANES = 8
LANES = 128


def _params(*sem):
    return pltpu.CompilerParams(dimension_semantics=sem, vmem_limit_bytes=VMEM_LIMIT_BYTES)


def _const_spec(shape):
    nd = len(shape)
    return pl.BlockSpec(shape, lambda *_: (0,) * nd)


def _rms(x, g):
    r = lax.rsqrt(jnp.mean(x * x, axis=-1, keepdims=True) + EPS)
    return x * r * g


def _log_sigmoid(z):
    return jnp.minimum(z, 0.0) - jnp.log1p(jnp.exp(-jnp.abs(z)))


def _split3(x):
    hi = x.astype(BF16)
    r = x - hi.astype(F32)
    mid = r.astype(BF16)
    lo = (r - mid.astype(F32)).astype(BF16)
    return hi, mid, lo


def _dot(a, b):
    return jnp.dot(a, b, preferred_element_type=F32)


def _dot_nt(a, b):
    return lax.dot_general(a, b, NT, preferred_element_type=F32)


def _conv_mixer_prompt_kernel(x_ref, win_ref, cw_ref, wout_ref, gpre_ref, gpost_ref,
                              h_ref, st_ref, ubuf, *, tm, nt):
    i = pl.program_id(1)

    @pl.when(i == 0)
    def _():
        ubuf[0:SUBLANES, :] = jnp.zeros((SUBLANES, D), F32)

    x = x_ref[...]
    a = _rms(x, gpre_ref[...]).astype(BF16)
    bcv = _dot(a, win_ref[...])
    u = bcv[:, D:2 * D] * bcv[:, 2 * D:]
    ubuf[SUBLANES:SUBLANES + tm, :] = u
    cw = cw_ref[...]
    y = (cw[0:1, :] * ubuf[SUBLANES - 2:SUBLANES - 2 + tm, :]
         + cw[1:2, :] * ubuf[SUBLANES - 1:SUBLANES - 1 + tm, :]
         + cw[2:3, :] * u)
    out = _dot((bcv[:, :D] * y).astype(BF16), wout_ref[...])
    h_ref[...] = x + _rms(out, gpost_ref[...])
    ubuf[0:SUBLANES, :] = ubuf[tm:tm + SUBLANES, :]

    @pl.when(i == nt - 1)
    def _():
        st_ref[0] = ubuf[SUBLANES - 2:SUBLANES, :]


def _conv_mixer_prompt(x, w_in, cw, w_out, g_pre, g_post, batch, seq, tm=512):
    nt = seq // tm
    return pl.pallas_call(
        functools.partial(_conv_mixer_prompt_kernel, tm=tm, nt=nt),
        grid=(batch, nt),
        in_specs=[
            pl.BlockSpec((tm, D), lambda b, i: (b * nt + i, 0)),
            _const_spec((D, 3 * D)),
            _const_spec((CONV_TAPS, D)),
            _const_spec((D, D)),
            _const_spec((1, D)),
            _const_spec((1, D)),
        ],
        out_specs=[
            pl.BlockSpec((tm, D), lambda b, i: (b * nt + i, 0)),
            pl.BlockSpec((1, CONV_TAPS - 1, D), lambda b, i: (b, 0, 0)),
        ],
        out_shape=[
            jax.ShapeDtypeStruct((batch * seq, D), F32),
            jax.ShapeDtypeStruct((batch, CONV_TAPS - 1, D), F32),
        ],
        scratch_shapes=[pltpu.VMEM((tm + SUBLANES, D), F32)],
        compiler_params=_params("arbitrary", "arbitrary"),
        name="conv_mixer_prompt",
    )(x, w_in, cw, w_out, g_pre, g_post)


def _conv_mixer_sample_kernel(x_ref, st0_ref, st1_ref, win_ref, cw_ref, wout_ref, gpre_ref,
                              gpost_ref, h_ref, st_ref, *, n, t):
    x = x_ref[...]
    a = _rms(x, gpre_ref[...]).astype(BF16)
    bcv = _dot(a, win_ref[...])
    u = bcv[:, D:2 * D] * bcv[:, 2 * D:]
    slabs = [st0_ref[...], st1_ref[...]] + [u[s * n:(s + 1) * n, :] for s in range(t)]
    cw = cw_ref[...]
    y = jnp.concatenate(
        [cw[0:1, :] * slabs[s] + cw[1:2, :] * slabs[s + 1] + cw[2:3, :] * slabs[s + 2]
         for s in range(t)], axis=0)
    out = _dot((bcv[:, :D] * y).astype(BF16), wout_ref[...])
    h_ref[...] = x + _rms(out, gpost_ref[...])
    st_ref[...] = jnp.concatenate(slabs[t:], axis=0)


def _conv_mixer_sample(x, st0, st1, w_in, cw, w_out, g_pre, g_post, n, t):
    rows = n * t
    shapes = [(rows, D), (n, D), (n, D), (D, 3 * D), (CONV_TAPS, D), (D, D), (1, D), (1, D)]
    return pl.pallas_call(
        functools.partial(_conv_mixer_sample_kernel, n=n, t=t),
        grid=(1,),
        in_specs=[_const_spec(s) for s in shapes],
        out_specs=[_const_spec((rows, D)), _const_spec(((CONV_TAPS - 1) * n, D))],
        out_shape=[
            jax.ShapeDtypeStruct((rows, D), F32),
            jax.ShapeDtypeStruct(((CONV_TAPS - 1) * n, D), F32),
        ],
        compiler_params=_params("arbitrary"),
        name="conv_mixer_sample",
    )(x, st0, st1, w_in, cw, w_out, g_pre, g_post)


def _mlp_kernel(h_ref, w1_ref, w2_ref, gpre_ref, gpost_ref, o_ref, *, n_chunks):
    h = h_ref[...]
    a = _rms(h, gpre_ref[...]).astype(BF16)
    ck = DFF // n_chunks
    acc = None
    for c in range(n_chunks):
        hm = jnp.maximum(_dot(a, w1_ref[:, c * ck:(c + 1) * ck]), 0.0)
        part = _dot((hm * hm).astype(BF16), w2_ref[c * ck:(c + 1) * ck, :])
        acc = part if acc is None else acc + part
    o_ref[...] = h + _rms(acc, gpost_ref[...])


def _mlp(h, w1, w2, g_pre, g_post, tm, name):
    rows = h.shape[0]
    return pl.pallas_call(
        functools.partial(_mlp_kernel, n_chunks=4),
        grid=(rows // tm,),
        in_specs=[
            pl.BlockSpec((tm, D), lambda i: (i, 0)),
            _const_spec((D, DFF)),
            _const_spec((DFF, D)),
            _const_spec((1, D)),
            _const_spec((1, D)),
        ],
        out_specs=pl.BlockSpec((tm, D), lambda i: (i, 0)),
        out_shape=jax.ShapeDtypeStruct((rows, D), F32),
        compiler_params=_params("arbitrary"),
        name=name,
    )(h, w1, w2, g_pre, g_post)


def _qkv_kernel(h_ref, wqkvt_ref, wft_ref, bft_ref, gpre_ref,
                qt_ref, kt_ref, vt_ref, krm_ref, vtb_ref, lft_ref, ft_ref, carry, *, tm, seq, q_scale):
    i = pl.program_id(1)
    a = _rms(h_ref[...], gpre_ref[...]).astype(BF16)
    qkvt = _dot_nt(wqkvt_ref[...], a)
    qt_ref[0] = (qkvt[:D, :] * q_scale).astype(BF16)
    kt = qkvt[D:2 * D, :]
    vt = qkvt[2 * D:, :]
    kt_ref[0] = kt
    vt_ref[0] = vt
    vtb_ref[0] = vt.astype(BF16)
    krm_ref[...] = kt.T.astype(BF16)

    lft = _log_sigmoid(_dot_nt(wft_ref[...], a) + bft_ref[...])
    lft_ref[0] = lft

    r = lax.broadcasted_iota(jnp.int32, (tm, tm), 0)
    c = lax.broadcasted_iota(jnp.int32, (tm, tm), 1)
    keep = r <= c
    if seq < tm:
        shift = seq.bit_length() - 1
        keep = keep & ((r >> shift) == (c >> shift))
    tri = jnp.where(keep, 1.0, 0.0).astype(BF16)
    hi, mid, lo = _split3(lft)
    f = _dot(hi, tri) + _dot(mid, tri) + _dot(lo, tri)
    if seq > tm:
        @pl.when(i == 0)
        def _():
            carry[...] = jnp.zeros((H, 1), F32)
        f = f + carry[...]
        carry[...] = f[:, tm - 1:tm]
    ft_ref[0] = f


def _qkv(h, w_qkvt, w_ft, b_ft, g_pre, batch, rows_per_batch, tm, seq, q_scale, name):
    assert seq & (seq - 1) == 0 and (seq == rows_per_batch if seq > tm else tm % seq == 0)
    nt = rows_per_batch // tm
    row_spec = pl.BlockSpec((tm, D), lambda b, i: (b * nt + i, 0))
    feat_spec = pl.BlockSpec((1, D, tm), lambda b, i: (b, 0, i))
    gate_spec = pl.BlockSpec((1, H, tm), lambda b, i: (b, 0, i))
    feat = lambda dt: jax.ShapeDtypeStruct((batch, D, rows_per_batch), dt)
    gate = jax.ShapeDtypeStruct((batch, H, rows_per_batch), F32)
    return pl.pallas_call(
        functools.partial(_qkv_kernel, tm=tm, seq=seq, q_scale=q_scale),
        grid=(batch, nt),
        in_specs=[row_spec, _const_spec((3 * D, D)), _const_spec((H, D)), _const_spec((H, 1)),
                  _const_spec((1, D))],
        out_specs=[feat_spec, feat_spec, feat_spec, row_spec, feat_spec, gate_spec, gate_spec],
        out_shape=[
            feat(BF16),
            feat(F32),
            feat(F32),
            jax.ShapeDtypeStruct((batch * rows_per_batch, D), BF16),
            feat(BF16),
            gate,
            gate,
        ],
        scratch_shapes=[pltpu.VMEM((H, 1), F32)],
        compiler_params=_params("arbitrary", "arbitrary"),
        name=name,
    )(h, w_qkvt, w_ft, b_ft, g_pre)


def _prompt_attn_keys(k_ref, ft_ref, kaug_ref, *, tq, seq):
    lane = lax.broadcasted_iota(jnp.int32, (tq, LANES), 1)

    def fill(c, carry):
        off = pl.multiple_of(c * tq, tq)
        kaug_ref[pl.ds(off, tq), 0:LANES] = k_ref[pl.ds(off, tq), :]
        aug = jnp.zeros((tq, LANES), F32)
        for hh in range(2):
            row = ft_ref[0, 0, hh:hh + 1, pl.ds(off, tq)] * (-LOG2E)
            col = jnp.broadcast_to(row, (LANES, tq)).T
            hi = col.astype(BF16).astype(F32)
            r1 = col - hi
            mid = r1.astype(BF16).astype(F32)
            lo = r1 - mid
            aug = jnp.where(lane == 3 * hh, hi,
                            jnp.where(lane == 3 * hh + 1, mid, jnp.where(lane == 3 * hh + 2, lo, aug)))
        kaug_ref[pl.ds(off, tq), LANES:2 * LANES] = aug.astype(BF16)
        return carry
    lax.fori_loop(0, seq // tq, fill, 0)


def _prompt_attn_tile(qi, qt_ref, vt_ref, o_ref, kaug_ref, acc_ref, s_ref, *, tq, before_loop, after_loop):
    qt = qt_ref[0]
    sub = lax.broadcasted_iota(jnp.int32, (2 * DH, tq), 0)
    zero = jnp.zeros_like(qt)
    q_heads = (jnp.where(sub < DH, qt, zero), jnp.where(sub >= DH, qt, zero))
    q_aug = []
    for hh in range(2):
        sel = jnp.where((sub >= 3 * hh) & (sub < 3 * hh + 3), 1.0, 0.0).astype(BF16)
        q_aug.append(jnp.concatenate([q_heads[hh], sel], axis=0))
    ones_rows = jnp.ones((2 * SUBLANES, tq), BF16)
    acc_ref[...] = jnp.zeros(acc_ref.shape, F32)
    neg = jnp.full((1, tq), -jnp.inf, F32)

    def scores(off):
        k = kaug_ref[pl.ds(off, tq), :]
        return tuple(_dot(k, q_aug[hh]) for hh in range(2))

    def consume(off, sts, stats, diagonal, before_head=(None, None)):
        vt = vt_ref[0, :, pl.ds(off, tq)]
        out = []
        for hh in range(2):
            if before_head[hh] is not None:
                before_head[hh]()
            m_prev = stats[hh]
            st = sts[hh]
            if diagonal:
                key = lax.broadcasted_iota(jnp.int32, (tq, tq), 0)
                qry = lax.broadcasted_iota(jnp.int32, (tq, tq), 1)
                st = jnp.where(key <= qry, st, -jnp.inf)
            m_new = jnp.maximum(m_prev, jnp.max(st, axis=0, keepdims=True))
            alpha = jnp.exp2(m_prev - m_new)
            p = jnp.exp2(st - m_new).astype(BF16)
            va = jnp.concatenate([vt[hh * DH:(hh + 1) * DH, :], ones_rows], axis=0)
            acc_ref[hh] = alpha * acc_ref[hh] + _dot(va, p)
            out.append(m_new)
        return tuple(out)

    def body(ki, stats):
        sts = (s_ref[0], s_ref[1])
        nxt = scores(pl.multiple_of((ki + 1) * tq, tq))
        stats = consume(pl.multiple_of(ki * tq, tq), sts, stats, False)
        s_ref[0] = nxt[0]
        s_ref[1] = nxt[1]
        return stats

    first = scores(0)
    s_ref[0] = first[0]
    s_ref[1] = first[1]
    handoff = before_loop()
    stats = lax.fori_loop(0, qi, body, (neg, neg))
    consume(pl.multiple_of(qi * tq, tq), (s_ref[0], s_ref[1]), stats, True, after_loop(handoff))
    a0, a1 = acc_ref[0], acc_ref[1]
    ot = jnp.concatenate([a0[:DH] / a0[DH:DH + 1], a1[:DH] / a1[DH:DH + 1]], axis=0)
    o_ref[...] = ot.T.astype(BF16)


def _proj_res_kernel(o_ref, w_ref, g_ref, h_ref, out_ref):
    m = _dot(o_ref[...].astype(BF16), w_ref[...])
    out_ref[...] = h_ref[...] + _rms(m, g_ref[...])


def _proj_res(o, w, g, h, tm, name):
    rows = h.shape[0]
    row_spec = pl.BlockSpec((tm, D), lambda i: (i, 0))
    return pl.pallas_call(
        _proj_res_kernel,
        grid=(rows // tm,),
        in_specs=[row_spec, _const_spec((D, D)), _const_spec((1, D)), row_spec],
        out_specs=row_spec,
        out_shape=jax.ShapeDtypeStruct((rows, D), F32),
        compiler_params=_params("arbitrary"),
        name=name,
    )(o, w, g, h)


def _side_by_side(page_refs):
    return jnp.concatenate([r[...].reshape(D, PAGE).astype(BF16) for r in page_refs], axis=1)


class _SampleAttn:
    def __init__(self, q_ref, k2_ref, v2_ref, fnt_ref, o_ref, qbd, m_ref, l_ref, acc_ref, carry, t):
        self.q_ref, self.k2_ref, self.v2_ref, self.fnt_ref, self.o_ref = q_ref, k2_ref, v2_ref, fnt_ref, o_ref
        self.qbd, self.m_ref, self.l_ref, self.acc_ref, self.carry, self.t = qbd, m_ref, l_ref, acc_ref, carry, t
        self.rows = t * H
        row = lax.broadcasted_iota(jnp.int32, (self.rows, D), 0)
        lane = lax.broadcasted_iota(jnp.int32, (self.rows, D), 1)
        self.own_head = (lane >> 6) == (row & (H - 1))

    def init(self):
        q = self.q_ref[0]
        rep = jnp.concatenate([jnp.broadcast_to(q[s:s + 1, :], (H, D)) for s in range(self.t)], axis=0)
        self.qbd[...] = jnp.where(self.own_head, rep, 0.0).astype(BF16)
        self.m_ref[...] = jnp.full(self.m_ref.shape, -jnp.inf, F32)
        self.l_ref[...] = jnp.zeros(self.l_ref.shape, F32)
        self.acc_ref[...] = jnp.zeros(self.acc_ref.shape, F32)
        self.carry[...] = jnp.zeros(self.carry.shape, F32)

    def _update(self, s, pv_fn):
        m_prev = self.m_ref[...]
        m_new = jnp.maximum(m_prev, jnp.max(s, axis=1, keepdims=True))
        alpha = jnp.exp(m_prev - m_new)
        pr = jnp.exp(s - m_new).astype(BF16)
        self.l_ref[...] = alpha * self.l_ref[...] + jnp.sum(pr.astype(F32), axis=1, keepdims=True)
        self.acc_ref[...] = alpha * self.acc_ref[...] + pv_fn(pr)
        self.m_ref[...] = m_new

    def page_scores(self, cl_refs, ck_refs):
        lpt = jnp.concatenate([r[...] for r in cl_refs], axis=0)
        j = lax.broadcasted_iota(jnp.int32, (PAGE, PAGE), 0)
        kk = lax.broadcasted_iota(jnp.int32, (PAGE, PAGE), 1)
        later = jnp.where(j > kk, 1.0, 0.0).astype(BF16)
        hi, mid, lo = _split3(lpt)
        within = _dot(hi, later) + _dot(mid, later) + _dot(lo, later)
        after = self.carry[...]
        bias = []
        for i in range(len(cl_refs)):
            bias.append(jnp.concatenate([within[i * H:(i + 1) * H, :] + after] * self.t, axis=0))
            after = after + jnp.sum(lpt[i * H:(i + 1) * H, :], axis=1, keepdims=True)
        self.carry[...] = after
        return _dot(self.qbd[...], _side_by_side(ck_refs)) + jnp.concatenate(bias, axis=1)

    def page_values(self, s, cv_refs):
        half = len(cv_refs) // 2
        state = {}

        def first():
            m_prev = self.m_ref[...]
            m_new = jnp.maximum(m_prev, jnp.max(s, axis=1, keepdims=True))
            state["alpha"] = jnp.exp(m_prev - m_new)
            state["pr"] = pr = jnp.exp(s - m_new).astype(BF16)
            self.m_ref[...] = m_new
            state["pv"] = _dot_nt(pr[:, :half * PAGE], _side_by_side(cv_refs[:half]))

        def second():
            alpha, pr = state["alpha"], state["pr"]
            pv = state["pv"] + _dot_nt(pr[:, half * PAGE:], _side_by_side(cv_refs[half:]))
            self.l_ref[...] = alpha * self.l_ref[...] + jnp.sum(pr.astype(F32), axis=1, keepdims=True)
            self.acc_ref[...] = alpha * self.acc_ref[...] + pv

        return first, second

    def finish(self):
        t, rows = self.t, self.rows
        pad = jnp.zeros((PAGE - SUBLANES, D), F32)
        k2 = jnp.concatenate([self.k2_ref[0], pad], axis=0).astype(BF16)
        v2 = jnp.concatenate([self.v2_ref[0], pad], axis=0).astype(BF16)
        s2 = _dot_nt(self.qbd[...], k2) - jnp.concatenate([self.fnt_ref[0]] * t, axis=0)
        r2 = lax.broadcasted_iota(jnp.int32, (rows, PAGE), 0)
        c2 = lax.broadcasted_iota(jnp.int32, (rows, PAGE), 1)
        s2 = jnp.where(c2 <= (r2 >> 4), s2, -jnp.inf)
        self._update(s2, lambda pr: _dot(pr, v2))
        o = jnp.where(self.own_head, self.acc_ref[...] / self.l_ref[...], 0.0)
        self.o_ref[0] = jnp.sum(o.reshape(t, H, D), axis=1)


def _attn_kernel(pt_ref, qt_ref, k_ref, vt_ref, ft_ref, q8_ref, k8_ref, v8_ref, fnt_ref, *refs,
                 tq, seq, pg, t):
    del pt_ref
    cl_refs, ck_refs, cv_refs = refs[:pg], refs[pg:2 * pg], refs[2 * pg:3 * pg]
    op_ref, os_ref, kaug, acc_p, s_buf, qbd, m_s, l_s, acc_s, carry = refs[3 * pg:]
    qi = pl.program_id(2)
    sample = _SampleAttn(q8_ref, k8_ref, v8_ref, fnt_ref, os_ref, qbd, m_s, l_s, acc_s, carry, t)

    @pl.when(qi == 0)
    def _():
        sample.init()
        _prompt_attn_keys(k_ref, ft_ref, kaug, tq=tq, seq=seq)

    _prompt_attn_tile(qi, qt_ref, vt_ref, op_ref, kaug, acc_p, s_buf, tq=tq,
                      before_loop=lambda: sample.page_scores(cl_refs, ck_refs),
                      after_loop=lambda s: sample.page_values(s, cv_refs))

    @pl.when(qi == seq // tq - 1)
    def _():
        sample.finish()


def _attn(qt, k_rm, vtb, ft, batch, seq, page_table, q8, k8, v8, fnt, cache_kt, cache_vt, cache_lt, t,
          tq=512, pg=8):
    n, n_pages = page_table.shape
    nq = seq // tq
    hp = H // 2
    assert n == batch * hp and n_pages == nq * pg, (n, batch, n_pages, nq, pg)
    rows = t * H
    seq_of = lambda b, h: b * hp + h
    tok_spec = pl.BlockSpec((1, SUBLANES, D), lambda b, h, i, pt: (seq_of(b, h), 0, 0))

    def page(j, nd):
        return lambda b, h, i, pt: (0, pt[seq_of(b, h), n_pages - 1 - (i * pg + j)]) + (0,) * nd

    grid_spec = pltpu.PrefetchScalarGridSpec(
        num_scalar_prefetch=1,
        grid=(batch, hp, nq),
        in_specs=(
            [pl.BlockSpec((1, 2 * DH, tq), lambda b, h, i, pt: (b, h, i)),
             pl.BlockSpec((seq, 2 * DH), lambda b, h, i, pt: (b, h)),
             pl.BlockSpec((1, 2 * DH, seq), lambda b, h, i, pt: (b, h, 0)),
             pl.BlockSpec((1, 1, 2, seq), lambda b, h, i, pt: (b, h, 0, 0)),
             tok_spec, tok_spec, tok_spec,
             pl.BlockSpec((1, H, PAGE), lambda b, h, i, pt: (seq_of(b, h), 0, 0))]
            + [pl.BlockSpec((None, None, H, PAGE), page(j, 2)) for j in range(pg)]
            + [pl.BlockSpec((None, None, H, DH, PAGE), page(j, 3)) for j in range(pg)]
            + [pl.BlockSpec((None, None, H, DH, PAGE), page(j, 3)) for j in range(pg)]
        ),
        out_specs=[
            pl.BlockSpec((tq, 2 * DH), lambda b, h, i, pt: (b * nq + i, h)),
            pl.BlockSpec((1, t, D), lambda b, h, i, pt: (seq_of(b, h), 0, 0)),
        ],
        scratch_shapes=[
            pltpu.VMEM((seq, 4 * DH), BF16),
            pltpu.VMEM((2, DH + 2 * SUBLANES, tq), F32),
            pltpu.VMEM((2, tq, tq), F32),
            pltpu.VMEM((rows, D), BF16),
            pltpu.VMEM((rows, 1), F32),
            pltpu.VMEM((rows, 1), F32),
            pltpu.VMEM((rows, D), F32),
            pltpu.VMEM((H, 1), F32),
        ],
    )
    return pl.pallas_call(
        functools.partial(_attn_kernel, tq=tq, seq=seq, pg=pg, t=t),
        grid_spec=grid_spec,
        out_shape=[jax.ShapeDtypeStruct((batch * seq, D), BF16), jax.ShapeDtypeStruct((n, t, D), F32)],
        compiler_params=_params("arbitrary", "arbitrary", "arbitrary"),
        name="attn",
    )(page_table, qt, k_rm, vtb, ft.reshape(batch, hp, 2, seq), q8, k8, v8, fnt,
      *([cache_lt] * pg), *([cache_kt] * pg), *([cache_vt] * pg))


def kernel(x_prompt, x_sample, state_conv, cache_k, cache_v, cache_logf, page_table, conv_w_in, conv_w, conv_w_out, attn_w_qkv, attn_w_f, attn_b_f, attn_w_o, mlp_w1, mlp_w2, norm_pre_mix, norm_post_mix, norm_pre_mlp, norm_post_mlp):
    batch, seq, _ = x_prompt.shape
    n, t, _ = x_sample.shape
    tm_p, tm_s = 512, n * t
    gain = lambda g, i: g[i].reshape(1, D)

    w_in, w_out = conv_w_in[0].astype(BF16), conv_w_out[0].astype(BF16)
    w1, w2 = mlp_w1[0].astype(BF16), mlp_w2[0].astype(BF16)
    hp, conv_p = _conv_mixer_prompt(x_prompt.reshape(batch * seq, D), w_in, conv_w[0], w_out,
                                    gain(norm_pre_mix, 0), gain(norm_post_mix, 0), batch, seq, tm_p)
    xs = x_sample.transpose(1, 0, 2).reshape(t * n, D)
    hs, conv_s = _conv_mixer_sample(xs, state_conv[0, :, 0], state_conv[0, :, 1], w_in, conv_w[0], w_out,
                                    gain(norm_pre_mix, 0), gain(norm_post_mix, 0), n, t)
    hp = _mlp(hp, w1, w2, gain(norm_pre_mlp, 0), gain(norm_post_mlp, 0), tm_p, "mlp0_prompt")
    hs = _mlp(hs, w1, w2, gain(norm_pre_mlp, 0), gain(norm_post_mlp, 0), tm_s, "mlp0_sample")
    hs = hs.reshape(t, n, D).transpose(1, 0, 2).reshape(n * t, D)

    w_qkvt, w_o = attn_w_qkv[0].T.astype(BF16), attn_w_o[0].astype(BF16)
    w_ft, b_ft = attn_w_f[0].T.astype(BF16), attn_b_f[0].reshape(H, 1)
    w1, w2 = mlp_w1[1].astype(BF16), mlp_w2[1].astype(BF16)

    qt, kt_p, vt_p, k_rm, vtb, lft_p, ft_p = _qkv(hp, w_qkvt, w_ft, b_ft, gain(norm_pre_mix, 1),
                                                  batch, seq, tm_p, seq, SCALE * LOG2E, "qkv_prompt")
    qt_s, kt_s, vt_s, _, _, lft_s, ft_s = _qkv(hs, w_qkvt, w_ft, b_ft, gain(norm_pre_mix, 1),
                                               1, n * t, tm_s, t, SCALE, "qkv_sample")
    k_s, v_s = kt_s[0].T, vt_s[0].T
    pad_rows = lambda a: jnp.pad(a.reshape(n, t, D), ((0, 0), (0, SUBLANES - t), (0, 0)))
    fnt = jnp.pad(ft_s[0].reshape(H, n, t).transpose(1, 0, 2), ((0, 0), (0, 0), (0, PAGE - t)))
    o_p, o_s = _attn(qt, k_rm, vtb, ft_p, batch, seq,
                     page_table, pad_rows(qt_s[0].T.astype(F32)), pad_rows(k_s), pad_rows(v_s), fnt,
                     cache_k.transpose(0, 1, 3, 4, 2), cache_v.transpose(0, 1, 3, 4, 2),
                     cache_logf.transpose(0, 1, 3, 2), t)
    hp = _proj_res(o_p, w_o, gain(norm_post_mix, 1), hp, tm_p, "attn_out_prompt")
    hp = _mlp(hp, w1, w2, gain(norm_pre_mlp, 1), gain(norm_post_mlp, 1), tm_p, "mlp1_prompt")
    hs = _proj_res(o_s.reshape(n * t, D), w_o, gain(norm_post_mix, 1), hs, tm_s, "attn_out_sample")
    hs = _mlp(hs, w1, w2, gain(norm_pre_mlp, 1), gain(norm_post_mlp, 1), tm_s, "mlp1_sample")

    heads_t = lambda a: a.reshape(1, batch, H, DH, seq).transpose(0, 1, 4, 2, 3)
    return (hp.reshape(batch, seq, D), hs.reshape(n, t, D),
            conv_p[None], conv_s.reshape(CONV_TAPS - 1, n, D).transpose(1, 0, 2)[None],
            heads_t(kt_p), heads_t(vt_p), lft_p.transpose(0, 2, 1)[None],
            k_s.reshape(1, n, t, H, DH), v_s.reshape(1, n, t, H, DH), lft_s[0].T.reshape(1, n, t, H))
```

```python
import functools

import jax
import jax.numpy as jnp
from jax import lax
from jax.experimental import pallas as pl
from jax.experimental.pallas import tpu as pltpu

D = 1024
H = 16
DH = 64
DFF = 4096
CONV_TAPS = 3
EPS = 1e-6
SCALE = 0.125
PAGE = 128
LOG2E = 1.4426950408889634

F32 = jnp.float32
BF16 = jnp.bfloat16
NT = (((1,), (1,)), ((), ()))

VMEM_LIMIT_BYTES = 56 * 1024 * 1024
SUBLANES = 8
LANES = 128


def _params(*sem):
    return pltpu.CompilerParams(dimension_semantics=sem, vmem_limit_bytes=VMEM_LIMIT_BYTES)


def _const_spec(shape):
    nd = len(shape)
    return pl.BlockSpec(shape, lambda *_: (0,) * nd)


def _rms(x, g):
    r = lax.rsqrt(jnp.mean(x * x, axis=-1, keepdims=True) + EPS)
    return x * r * g


def _log_sigmoid(z):
    return jnp.minimum(z, 0.0) - jnp.log1p(jnp.exp(-jnp.abs(z)))


def _split3(x):
    hi = x.astype(BF16)
    r = x - hi.astype(F32)
    mid = r.astype(BF16)
    lo = (r - mid.astype(F32)).astype(BF16)
    return hi, mid, lo


def _dot(a, b):
    return jnp.dot(a, b, preferred_element_type=F32)


def _dot_nt(a, b):
    return lax.dot_general(a, b, NT, preferred_element_type=F32)


def _conv_mixer_prompt_kernel(x_ref, win_ref, cw_ref, wout_ref, gpre_ref, gpost_ref,
                              h_ref, st_ref, ubuf, *, tm, nt):
    i = pl.program_id(1)

    @pl.when(i == 0)
    def _():
        ubuf[0:SUBLANES, :] = jnp.zeros((SUBLANES, D), F32)

    x = x_ref[...]
    a = _rms(x, gpre_ref[...]).astype(BF16)
    bcv = _dot(a, win_ref[...])
    u = bcv[:, D:2 * D] * bcv[:, 2 * D:]
    ubuf[SUBLANES:SUBLANES + tm, :] = u
    cw = cw_ref[...]
    y = (cw[0:1, :] * ubuf[SUBLANES - 2:SUBLANES - 2 + tm, :]
         + cw[1:2, :] * ubuf[SUBLANES - 1:SUBLANES - 1 + tm, :]
         + cw[2:3, :] * u)
    out = _dot((bcv[:, :D] * y).astype(BF16), wout_ref[...])
    h_ref[...] = x + _rms(out, gpost_ref[...])
    ubuf[0:SUBLANES, :] = ubuf[tm:tm + SUBLANES, :]

    @pl.when(i == nt - 1)
    def _():
        st_ref[0] = ubuf[SUBLANES - 2:SUBLANES, :]


def _conv_mixer_prompt(x, w_in, cw, w_out, g_pre, g_post, batch, seq, tm=512):
    nt = seq // tm
    return pl.pallas_call(
        functools.partial(_conv_mixer_prompt_kernel, tm=tm, nt=nt),
        grid=(batch, nt),
        in_specs=[
            pl.BlockSpec((tm, D), lambda b, i: (b * nt + i, 0)),
            _const_spec((D, 3 * D)),
            _const_spec((CONV_TAPS, D)),
            _const_spec((D, D)),
            _const_spec((1, D)),
            _const_spec((1, D)),
        ],
        out_specs=[
            pl.BlockSpec((tm, D), lambda b, i: (b * nt + i, 0)),
            pl.BlockSpec((1, CONV_TAPS - 1, D), lambda b, i: (b, 0, 0)),
        ],
        out_shape=[
            jax.ShapeDtypeStruct((batch * seq, D), F32),
            jax.ShapeDtypeStruct((batch, CONV_TAPS - 1, D), F32),
        ],
        scratch_shapes=[pltpu.VMEM((tm + SUBLANES, D), F32)],
        compiler_params=_params("arbitrary", "arbitrary"),
        name="conv_mixer_prompt",
    )(x, w_in, cw, w_out, g_pre, g_post)


def _conv_mixer_sample_kernel(x_ref, st0_ref, st1_ref, win_ref, cw_ref, wout_ref, gpre_ref,
                              gpost_ref, h_ref, st_ref, *, n, t):
    x = x_ref[...]
    a = _rms(x, gpre_ref[...]).astype(BF16)
    bcv = _dot(a, win_ref[...])
    u = bcv[:, D:2 * D] * bcv[:, 2 * D:]
    slabs = [st0_ref[...], st1_ref[...]] + [u[s * n:(s + 1) * n, :] for s in range(t)]
    cw = cw_ref[...]
    y = jnp.concatenate(
        [cw[0:1, :] * slabs[s] + cw[1:2, :] * slabs[s + 1] + cw[2:3, :] * slabs[s + 2]
         for s in range(t)], axis=0)
    out = _dot((bcv[:, :D] * y).astype(BF16), wout_ref[...])
    h_ref[...] = x + _rms(out, gpost_ref[...])
    st_ref[...] = jnp.concatenate(slabs[t:], axis=0)


def _conv_mixer_sample(x, st0, st1, w_in, cw, w_out, g_pre, g_post, n, t):
    rows = n * t
    shapes = [(rows, D), (n, D), (n, D), (D, 3 * D), (CONV_TAPS, D), (D, D), (1, D), (1, D)]
    return pl.pallas_call(
        functools.partial(_conv_mixer_sample_kernel, n=n, t=t),
        grid=(1,),
        in_specs=[_const_spec(s) for s in shapes],
        out_specs=[_const_spec((rows, D)), _const_spec(((CONV_TAPS - 1) * n, D))],
        out_shape=[
            jax.ShapeDtypeStruct((rows, D), F32),
            jax.ShapeDtypeStruct(((CONV_TAPS - 1) * n, D), F32),
        ],
        compiler_params=_params("arbitrary"),
        name="conv_mixer_sample",
    )(x, st0, st1, w_in, cw, w_out, g_pre, g_post)


def _mlp_tail(h, w1_ref, w2_ref, gpre_ref, gpost_ref, n_chunks):
    a = _rms(h, gpre_ref[...]).astype(BF16)
    ck = DFF // n_chunks
    acc = None
    for c in range(n_chunks):
        hm = jnp.maximum(_dot(a, w1_ref[:, c * ck:(c + 1) * ck]), 0.0)
        part = _dot((hm * hm).astype(BF16), w2_ref[c * ck:(c + 1) * ck, :])
        acc = part if acc is None else acc + part
    return h + _rms(acc, gpost_ref[...])


def _attn_out_mlp_kernel(o_ref, wo_ref, gmix_ref, h_ref, w1_ref, w2_ref, gpre_ref, gpost_ref, out_ref, *,
                         n_chunks):
    h = h_ref[...] + _rms(_dot(o_ref[...].astype(BF16), wo_ref[...]), gmix_ref[...])
    out_ref[...] = _mlp_tail(h, w1_ref, w2_ref, gpre_ref, gpost_ref, n_chunks)


def _attn_out_mlp(o, w_o, g_mix, h, w1, w2, g_pre, g_post, tm, name):
    rows = h.shape[0]
    row_spec = pl.BlockSpec((tm, D), lambda i: (i, 0))
    return pl.pallas_call(
        functools.partial(_attn_out_mlp_kernel, n_chunks=4),
        grid=(rows // tm,),
        in_specs=[row_spec, _const_spec((D, D)), _const_spec((1, D)), row_spec,
                  _const_spec((D, DFF)), _const_spec((DFF, D)), _const_spec((1, D)), _const_spec((1, D))],
        out_specs=row_spec,
        out_shape=jax.ShapeDtypeStruct((rows, D), F32),
        compiler_params=_params("arbitrary"),
        name=name,
    )(o, w_o, g_mix, h, w1, w2, g_pre, g_post)


def _mlp_kernel(h_ref, w1_ref, w2_ref, gpre_ref, gpost_ref, o_ref, *, n_chunks):
    o_ref[...] = _mlp_tail(h_ref[...], w1_ref, w2_ref, gpre_ref, gpost_ref, n_chunks)


def _mlp(h, w1, w2, g_pre, g_post, tm, name):
    rows = h.shape[0]
    return pl.pallas_call(
        functools.partial(_mlp_kernel, n_chunks=4),
        grid=(rows // tm,),
        in_specs=[
            pl.BlockSpec((tm, D), lambda i: (i, 0)),
            _const_spec((D, DFF)),
            _const_spec((DFF, D)),
            _const_spec((1, D)),
            _const_spec((1, D)),
        ],
        out_specs=pl.BlockSpec((tm, D), lambda i: (i, 0)),
        out_shape=jax.ShapeDtypeStruct((rows, D), F32),
        compiler_params=_params("arbitrary"),
        name=name,
    )(h, w1, w2, g_pre, g_post)


def _qkv_kernel(h_ref, wqkvt_ref, wft_ref, bft_ref, gpre_ref,
                qt_ref, kt_ref, vt_ref, krm_ref, vtb_ref, lft_ref, ft_ref, carry, *, tm, seq, q_scale):
    i = pl.program_id(1)
    a = _rms(h_ref[...], gpre_ref[...]).astype(BF16)
    qkvt = _dot_nt(wqkvt_ref[...], a)
    qt_ref[0] = (qkvt[:D, :] * q_scale).astype(BF16)
    kt = qkvt[D:2 * D, :]
    vt = qkvt[2 * D:, :]
    kt_ref[0] = kt
    vt_ref[0] = vt
    vtb_ref[0] = vt.astype(BF16)
    krm_ref[...] = kt.T.astype(BF16)

    lft = _log_sigmoid(_dot_nt(wft_ref[...], a) + bft_ref[...])
    lft_ref[0] = lft

    r = lax.broadcasted_iota(jnp.int32, (tm, tm), 0)
    c = lax.broadcasted_iota(jnp.int32, (tm, tm), 1)
    keep = r <= c
    if seq < tm:
        shift = seq.bit_length() - 1
        keep = keep & ((r >> shift) == (c >> shift))
    tri = jnp.where(keep, 1.0, 0.0).astype(BF16)
    hi, mid, lo = _split3(lft)
    f = _dot(hi, tri) + _dot(mid, tri) + _dot(lo, tri)
    if seq > tm:
        @pl.when(i == 0)
        def _():
            carry[...] = jnp.zeros((H, 1), F32)
        f = f + carry[...]
        carry[...] = f[:, tm - 1:tm]
    ft_ref[0] = f


def _qkv(h, w_qkvt, w_ft, b_ft, g_pre, batch, rows_per_batch, tm, seq, q_scale, name):
    assert seq & (seq - 1) == 0 and (seq == rows_per_batch if seq > tm else tm % seq == 0)
    nt = rows_per_batch // tm
    row_spec = pl.BlockSpec((tm, D), lambda b, i: (b * nt + i, 0))
    feat_spec = pl.BlockSpec((1, D, tm), lambda b, i: (b, 0, i))
    gate_spec = pl.BlockSpec((1, H, tm), lambda b, i: (b, 0, i))
    feat = lambda dt: jax.ShapeDtypeStruct((batch, D, rows_per_batch), dt)
    gate = jax.ShapeDtypeStruct((batch, H, rows_per_batch), F32)
    return pl.pallas_call(
        functools.partial(_qkv_kernel, tm=tm, seq=seq, q_scale=q_scale),
        grid=(batch, nt),
        in_specs=[row_spec, _const_spec((3 * D, D)), _const_spec((H, D)), _const_spec((H, 1)),
                  _const_spec((1, D))],
        out_specs=[feat_spec, feat_spec, feat_spec, row_spec, feat_spec, gate_spec, gate_spec],
        out_shape=[
            feat(BF16),
            feat(F32),
            feat(F32),
            jax.ShapeDtypeStruct((batch * rows_per_batch, D), BF16),
            feat(BF16),
            gate,
            gate,
        ],
        scratch_shapes=[pltpu.VMEM((H, 1), F32)],
        compiler_params=_params("arbitrary", "arbitrary"),
        name=name,
    )(h, w_qkvt, w_ft, b_ft, g_pre)


def _prompt_attn_keys(k_ref, ft_ref, kaug_ref, *, tq, seq):
    lane = lax.broadcasted_iota(jnp.int32, (tq, LANES), 1)

    def fill(c, carry):
        off = pl.multiple_of(c * tq, tq)
        kaug_ref[pl.ds(off, tq), 0:LANES] = k_ref[pl.ds(off, tq), :]
        aug = jnp.zeros((tq, LANES), F32)
        for hh in range(2):
            row = ft_ref[0, 0, hh:hh + 1, pl.ds(off, tq)] * (-LOG2E)
            col = jnp.broadcast_to(row, (LANES, tq)).T
            hi = col.astype(BF16).astype(F32)
            r1 = col - hi
            mid = r1.astype(BF16).astype(F32)
            lo = r1 - mid
            aug = jnp.where(lane == 3 * hh, hi,
                            jnp.where(lane == 3 * hh + 1, mid, jnp.where(lane == 3 * hh + 2, lo, aug)))
        kaug_ref[pl.ds(off, tq), LANES:2 * LANES] = aug.astype(BF16)
        return carry
    lax.fori_loop(0, seq // tq, fill, 0)


def _prompt_attn_tile(qi, qt_ref, vt_ref, o_ref, kaug_ref, acc_ref, s_ref, *, tq, before_loop, after_loop):
    qt = qt_ref[0]
    sub = lax.broadcasted_iota(jnp.int32, (2 * DH, tq), 0)
    zero = jnp.zeros_like(qt)
    q_heads = (jnp.where(sub < DH, qt, zero), jnp.where(sub >= DH, qt, zero))
    q_aug = []
    for hh in range(2):
        sel = jnp.where((sub >= 3 * hh) & (sub < 3 * hh + 3), 1.0, 0.0).astype(BF16)
        q_aug.append(jnp.concatenate([q_heads[hh], sel], axis=0))
    ones_rows = jnp.ones((2 * SUBLANES, tq), BF16)
    acc_ref[...] = jnp.zeros(acc_ref.shape, F32)
    neg = jnp.full((1, tq), -jnp.inf, F32)

    def scores(off):
        k = kaug_ref[pl.ds(off, tq), :]
        return tuple(_dot(k, q_aug[hh]) for hh in range(2))

    def consume(off, sts, stats, diagonal, before_head=(None, None)):
        vt = vt_ref[0, :, pl.ds(off, tq)]
        out = []
        for hh in range(2):
            if before_head[hh] is not None:
                before_head[hh]()
            m_prev = stats[hh]
            st = sts[hh]
            if diagonal:
                key = lax.broadcasted_iota(jnp.int32, (tq, tq), 0)
                qry = lax.broadcasted_iota(jnp.int32, (tq, tq), 1)
                st = jnp.where(key <= qry, st, -jnp.inf)
            m_new = jnp.maximum(m_prev, jnp.max(st, axis=0, keepdims=True))
            alpha = jnp.exp2(m_prev - m_new)
            p = jnp.exp2(st - m_new).astype(BF16)
            va = jnp.concatenate([vt[hh * DH:(hh + 1) * DH, :], ones_rows], axis=0)
            acc_ref[hh] = alpha * acc_ref[hh] + _dot(va, p)
            out.append(m_new)
        return tuple(out)

    def step(blk, stats, src, dst):
        sts = (s_ref[src, 0], s_ref[src, 1])
        nxt = scores(pl.multiple_of((blk + 1) * tq, tq))
        stats = consume(pl.multiple_of(blk * tq, tq), sts, stats, False)
        s_ref[dst, 0] = nxt[0]
        s_ref[dst, 1] = nxt[1]
        return stats

    def body(j, stats):
        return step(2 * j + 1, step(2 * j, stats, 0, 1), 1, 0)

    first = scores(0)
    s_ref[0, 0] = first[0]
    s_ref[0, 1] = first[1]
    handoff = before_loop()
    stats = lax.fori_loop(0, qi // 2, body, (neg, neg))
    stats = lax.cond(qi % 2 == 1, lambda st: step(qi - 1, st, 0, 1), lambda st: st, stats)
    last = qi % 2
    consume(pl.multiple_of(qi * tq, tq), (s_ref[last, 0], s_ref[last, 1]), stats, True, after_loop(handoff))
    a0, a1 = acc_ref[0], acc_ref[1]
    ot = jnp.concatenate([a0[:DH] / a0[DH:DH + 1], a1[:DH] / a1[DH:DH + 1]], axis=0)
    o_ref[...] = ot.T.astype(BF16)


def _side_by_side(page_refs):
    return jnp.concatenate([r[...].reshape(D, PAGE).astype(BF16) for r in page_refs], axis=1)


class _SampleAttn:
    def __init__(self, q_ref, k2_ref, v2_ref, fnt_ref, o_ref, qbd, m_ref, l_ref, acc_ref, carry, t):
        self.q_ref, self.k2_ref, self.v2_ref, self.fnt_ref, self.o_ref = q_ref, k2_ref, v2_ref, fnt_ref, o_ref
        self.qbd, self.m_ref, self.l_ref, self.acc_ref, self.carry, self.t = qbd, m_ref, l_ref, acc_ref, carry, t
        self.rows = t * H
        row = lax.broadcasted_iota(jnp.int32, (self.rows, D), 0)
        lane = lax.broadcasted_iota(jnp.int32, (self.rows, D), 1)
        self.own_head = (lane >> 6) == (row & (H - 1))

    def init(self):
        q = self.q_ref[0]
        rep = jnp.concatenate([jnp.broadcast_to(q[s:s + 1, :], (H, D)) for s in range(self.t)], axis=0)
        self.qbd[...] = jnp.where(self.own_head, rep, 0.0).astype(BF16)
        self.m_ref[...] = jnp.full(self.m_ref.shape, -jnp.inf, F32)
        self.l_ref[...] = jnp.zeros(self.l_ref.shape, F32)
        self.acc_ref[...] = jnp.zeros(self.acc_ref.shape, F32)
        self.carry[...] = jnp.zeros(self.carry.shape, F32)

    def _update(self, s, pv_fn):
        m_prev = self.m_ref[...]
        m_new = jnp.maximum(m_prev, jnp.max(s, axis=1, keepdims=True))
        alpha = jnp.exp(m_prev - m_new)
        pr = jnp.exp(s - m_new).astype(BF16)
        self.l_ref[...] = alpha * self.l_ref[...] + jnp.sum(pr.astype(F32), axis=1, keepdims=True)
        self.acc_ref[...] = alpha * self.acc_ref[...] + pv_fn(pr)
        self.m_ref[...] = m_new

    def page_scores(self, cl_refs, ck_refs):
        lpt = jnp.concatenate([r[...] for r in cl_refs], axis=0)
        j = lax.broadcasted_iota(jnp.int32, (PAGE, PAGE), 0)
        kk = lax.broadcasted_iota(jnp.int32, (PAGE, PAGE), 1)
        later = jnp.where(j > kk, 1.0, 0.0).astype(BF16)
        hi, mid, lo = _split3(lpt)
        within = _dot(hi, later) + _dot(mid, later) + _dot(lo, later)
        after = self.carry[...]
        bias = []
        for i in range(len(cl_refs)):
            bias.append(jnp.concatenate([within[i * H:(i + 1) * H, :] + after] * self.t, axis=0))
            after = after + jnp.sum(lpt[i * H:(i + 1) * H, :], axis=1, keepdims=True)
        self.carry[...] = after
        return _dot(self.qbd[...], _side_by_side(ck_refs)) + jnp.concatenate(bias, axis=1)

    def page_values(self, s, cv_refs):
        half = len(cv_refs) // 2
        state = {}

        def first():
            m_prev = self.m_ref[...]
            m_new = jnp.maximum(m_prev, jnp.max(s, axis=1, keepdims=True))
            state["alpha"] = jnp.exp(m_prev - m_new)
            state["pr"] = pr = jnp.exp(s - m_new).astype(BF16)
            self.m_ref[...] = m_new
            state["pv"] = _dot_nt(pr[:, :half * PAGE], _side_by_side(cv_refs[:half]))

        def second():
            alpha, pr = state["alpha"], state["pr"]
            pv = state["pv"] + _dot_nt(pr[:, half * PAGE:], _side_by_side(cv_refs[half:]))
            self.l_ref[...] = alpha * self.l_ref[...] + jnp.sum(pr.astype(F32), axis=1, keepdims=True)
            self.acc_ref[...] = alpha * self.acc_ref[...] + pv

        return first, second

    def finish(self):
        t, rows = self.t, self.rows
        pad = jnp.zeros((PAGE - SUBLANES, D), F32)
        k2 = jnp.concatenate([self.k2_ref[0], pad], axis=0).astype(BF16)
        v2 = jnp.concatenate([self.v2_ref[0], pad], axis=0).astype(BF16)
        s2 = _dot_nt(self.qbd[...], k2) - jnp.concatenate([self.fnt_ref[0]] * t, axis=0)
        r2 = lax.broadcasted_iota(jnp.int32, (rows, PAGE), 0)
        c2 = lax.broadcasted_iota(jnp.int32, (rows, PAGE), 1)
        s2 = jnp.where(c2 <= (r2 >> 4), s2, -jnp.inf)
        self._update(s2, lambda pr: _dot(pr, v2))
        o = jnp.where(self.own_head, self.acc_ref[...] / self.l_ref[...], 0.0)
        self.o_ref[0] = jnp.sum(o.reshape(t, H, D), axis=1)


def _attn_kernel(pt_ref, qt_ref, k_ref, vt_ref, ft_ref, q8_ref, k8_ref, v8_ref, fnt_ref, *refs,
                 tq, seq, pg, t):
    del pt_ref
    cl_refs, ck_refs, cv_refs = refs[:pg], refs[pg:2 * pg], refs[2 * pg:3 * pg]
    op_ref, os_ref, kaug, acc_p, s_buf, qbd, m_s, l_s, acc_s, carry = refs[3 * pg:]
    qi = pl.program_id(2)
    sample = _SampleAttn(q8_ref, k8_ref, v8_ref, fnt_ref, os_ref, qbd, m_s, l_s, acc_s, carry, t)

    @pl.when(qi == 0)
    def _():
        sample.init()
        _prompt_attn_keys(k_ref, ft_ref, kaug, tq=tq, seq=seq)

    _prompt_attn_tile(qi, qt_ref, vt_ref, op_ref, kaug, acc_p, s_buf, tq=tq,
                      before_loop=lambda: sample.page_scores(cl_refs, ck_refs),
                      after_loop=lambda s: sample.page_values(s, cv_refs))

    @pl.when(qi == seq // tq - 1)
    def _():
        sample.finish()


def _attn(qt, k_rm, vtb, ft, batch, seq, page_table, q8, k8, v8, fnt, cache_kt, cache_vt, cache_lt, t,
          tq=512, pg=8):
    n, n_pages = page_table.shape
    nq = seq // tq
    hp = H // 2
    assert n == batch * hp and n_pages == nq * pg, (n, batch, n_pages, nq, pg)
    rows = t * H
    seq_of = lambda b, h: b * hp + h
    tok_spec = pl.BlockSpec((1, SUBLANES, D), lambda b, h, i, pt: (seq_of(b, h), 0, 0))

    def page(j, nd):
        return lambda b, h, i, pt: (0, pt[seq_of(b, h), n_pages - 1 - (i * pg + j)]) + (0,) * nd

    grid_spec = pltpu.PrefetchScalarGridSpec(
        num_scalar_prefetch=1,
        grid=(batch, hp, nq),
        in_specs=(
            [pl.BlockSpec((1, 2 * DH, tq), lambda b, h, i, pt: (b, h, i)),
             pl.BlockSpec((seq, 2 * DH), lambda b, h, i, pt: (b, h)),
             pl.BlockSpec((1, 2 * DH, seq), lambda b, h, i, pt: (b, h, 0)),
             pl.BlockSpec((1, 1, 2, seq), lambda b, h, i, pt: (b, h, 0, 0)),
             tok_spec, tok_spec, tok_spec,
             pl.BlockSpec((1, H, PAGE), lambda b, h, i, pt: (seq_of(b, h), 0, 0))]
            + [pl.BlockSpec((None, None, H, PAGE), page(j, 2)) for j in range(pg)]
            + [pl.BlockSpec((None, None, H, DH, PAGE), page(j, 3)) for j in range(pg)]
            + [pl.BlockSpec((None, None, H, DH, PAGE), page(j, 3)) for j in range(pg)]
        ),
        out_specs=[
            pl.BlockSpec((tq, 2 * DH), lambda b, h, i, pt: (b * nq + i, h)),
            pl.BlockSpec((1, t, D), lambda b, h, i, pt: (seq_of(b, h), 0, 0)),
        ],
        scratch_shapes=[
            pltpu.VMEM((seq, 4 * DH), BF16),
            pltpu.VMEM((2, DH + 2 * SUBLANES, tq), F32),
            pltpu.VMEM((2, 2, tq, tq), F32),
            pltpu.VMEM((rows, D), BF16),
            pltpu.VMEM((rows, 1), F32),
            pltpu.VMEM((rows, 1), F32),
            pltpu.VMEM((rows, D), F32),
            pltpu.VMEM((H, 1), F32),
        ],
    )
    return pl.pallas_call(
        functools.partial(_attn_kernel, tq=tq, seq=seq, pg=pg, t=t),
        grid_spec=grid_spec,
        out_shape=[jax.ShapeDtypeStruct((batch * seq, D), BF16), jax.ShapeDtypeStruct((n, t, D), F32)],
        compiler_params=_params("arbitrary", "arbitrary", "arbitrary"),
        name="attn",
    )(page_table, qt, k_rm, vtb, ft.reshape(batch, hp, 2, seq), q8, k8, v8, fnt,
      *([cache_lt] * pg), *([cache_kt] * pg), *([cache_vt] * pg))


def kernel(x_prompt, x_sample, state_conv, cache_k, cache_v, cache_logf, page_table, conv_w_in, conv_w, conv_w_out, attn_w_qkv, attn_w_f, attn_b_f, attn_w_o, mlp_w1, mlp_w2, norm_pre_mix, norm_post_mix, norm_pre_mlp, norm_post_mlp):
    batch, seq, _ = x_prompt.shape
    n, t, _ = x_sample.shape
    tm_p, tm_s = 512, n * t
    gain = lambda g, i: g[i].reshape(1, D)

    w_in, w_out = conv_w_in[0].astype(BF16), conv_w_out[0].astype(BF16)
    w1, w2 = mlp_w1[0].astype(BF16), mlp_w2[0].astype(BF16)
    hp, conv_p = _conv_mixer_prompt(x_prompt.reshape(batch * seq, D), w_in, conv_w[0], w_out,
                                    gain(norm_pre_mix, 0), gain(norm_post_mix, 0), batch, seq, tm_p)
    xs = x_sample.transpose(1, 0, 2).reshape(t * n, D)
    hs, conv_s = _conv_mixer_sample(xs, state_conv[0, :, 0], state_conv[0, :, 1], w_in, conv_w[0], w_out,
                                    gain(norm_pre_mix, 0), gain(norm_post_mix, 0), n, t)
    hp = _mlp(hp, w1, w2, gain(norm_pre_mlp, 0), gain(norm_post_mlp, 0), tm_p, "mlp0_prompt")
    hs = _mlp(hs, w1, w2, gain(norm_pre_mlp, 0), gain(norm_post_mlp, 0), tm_s, "mlp0_sample")
    hs = hs.reshape(t, n, D).transpose(1, 0, 2).reshape(n * t, D)

    w_qkvt, w_o = attn_w_qkv[0].T.astype(BF16), attn_w_o[0].astype(BF16)
    w_ft, b_ft = attn_w_f[0].T.astype(BF16), attn_b_f[0].reshape(H, 1)
    w1, w2 = mlp_w1[1].astype(BF16), mlp_w2[1].astype(BF16)

    qt, kt_p, vt_p, k_rm, vtb, lft_p, ft_p = _qkv(hp, w_qkvt, w_ft, b_ft, gain(norm_pre_mix, 1),
                                                  batch, seq, tm_p, seq, SCALE * LOG2E, "qkv_prompt")
    qt_s, kt_s, vt_s, _, _, lft_s, ft_s = _qkv(hs, w_qkvt, w_ft, b_ft, gain(norm_pre_mix, 1),
                                               1, n * t, tm_s, t, SCALE, "qkv_sample")
    k_s, v_s = kt_s[0].T, vt_s[0].T
    pad_rows = lambda a: jnp.pad(a.reshape(n, t, D), ((0, 0), (0, SUBLANES - t), (0, 0)))
    fnt = jnp.pad(ft_s[0].reshape(H, n, t).transpose(1, 0, 2), ((0, 0), (0, 0), (0, PAGE - t)))
    o_p, o_s = _attn(qt, k_rm, vtb, ft_p, batch, seq,
                     page_table, pad_rows(qt_s[0].T.astype(F32)), pad_rows(k_s), pad_rows(v_s), fnt,
                     cache_k.transpose(0, 1, 3, 4, 2), cache_v.transpose(0, 1, 3, 4, 2),
                     cache_logf.transpose(0, 1, 3, 2), t)
    hp = _attn_out_mlp(o_p, w_o, gain(norm_post_mix, 1), hp, w1, w2,
                       gain(norm_pre_mlp, 1), gain(norm_post_mlp, 1), tm_p, "attn_out_mlp1_prompt")
    hs = _attn_out_mlp(o_s.reshape(n * t, D), w_o, gain(norm_post_mix, 1), hs, w1, w2,
                       gain(norm_pre_mlp, 1), gain(norm_post_mlp, 1), tm_s, "attn_out_mlp1_sample")

    heads_t = lambda a: a.reshape(1, batch, H, DH, seq).transpose(0, 1, 4, 2, 3)
    return (hp.reshape(batch, seq, D), hs.reshape(n, t, D),
            conv_p[None], conv_s.reshape(CONV_TAPS - 1, n, D).transpose(1, 0, 2)[None],
            heads_t(kt_p), heads_t(vt_p), lft_p.transpose(0, 2, 1)[None],
            k_s.reshape(1, n, t, H, DH), v_s.reshape(1, n, t, H, DH), lft_s[0].T.reshape(1, n, t, H))
```

```python
import functools

import jax
import jax.numpy as jnp
from jax import lax
from jax.experimental import pallas as pl
from jax.experimental.pallas import tpu as pltpu

D = 1024
H = 16
DH = 64
DFF = 4096
CONV_TAPS = 3
EPS = 1e-6
SCALE = 0.125
PAGE = 128
LOG2E = 1.4426950408889634
KV_UNROLL = 4

F32 = jnp.float32
BF16 = jnp.bfloat16
NT = (((1,), (1,)), ((), ()))

VMEM_LIMIT_BYTES = 56 * 1024 * 1024
SUBLANES = 8
LANES = 128


def _params(*sem):
    return pltpu.CompilerParams(dimension_semantics=sem, vmem_limit_bytes=VMEM_LIMIT_BYTES)


def _const_spec(shape):
    nd = len(shape)
    return pl.BlockSpec(shape, lambda *_: (0,) * nd)


def _rms(x, g):
    r = lax.rsqrt(jnp.mean(x * x, axis=-1, keepdims=True) + EPS)
    return x * r * g


def _log_sigmoid(z):
    return jnp.minimum(z, 0.0) - jnp.log1p(jnp.exp(-jnp.abs(z)))


def _split3(x):
    hi = x.astype(BF16)
    r = x - hi.astype(F32)
    mid = r.astype(BF16)
    lo = (r - mid.astype(F32)).astype(BF16)
    return hi, mid, lo


def _dot(a, b):
    return jnp.dot(a, b, preferred_element_type=F32)


def _dot_nt(a, b):
    return lax.dot_general(a, b, NT, preferred_element_type=F32)


def _conv_mixer_prompt_kernel(x_ref, win_ref, cw_ref, wout_ref, gpre_ref, gpost_ref,
                              h_ref, st_ref, ubuf, *, tm, nt):
    i = pl.program_id(1)

    @pl.when(i == 0)
    def _():
        ubuf[0:SUBLANES, :] = jnp.zeros((SUBLANES, D), F32)

    x = x_ref[...]
    a = _rms(x, gpre_ref[...]).astype(BF16)
    bcv = _dot(a, win_ref[...])
    u = bcv[:, D:2 * D] * bcv[:, 2 * D:]
    ubuf[SUBLANES:SUBLANES + tm, :] = u
    cw = cw_ref[...]
    y = (cw[0:1, :] * ubuf[SUBLANES - 2:SUBLANES - 2 + tm, :]
         + cw[1:2, :] * ubuf[SUBLANES - 1:SUBLANES - 1 + tm, :]
         + cw[2:3, :] * u)
    out = _dot((bcv[:, :D] * y).astype(BF16), wout_ref[...])
    h_ref[...] = x + _rms(out, gpost_ref[...])
    ubuf[0:SUBLANES, :] = ubuf[tm:tm + SUBLANES, :]

    @pl.when(i == nt - 1)
    def _():
        st_ref[0] = ubuf[SUBLANES - 2:SUBLANES, :]


def _conv_mixer_prompt(x, w_in, cw, w_out, g_pre, g_post, batch, seq, tm=512):
    nt = seq // tm
    return pl.pallas_call(
        functools.partial(_conv_mixer_prompt_kernel, tm=tm, nt=nt),
        grid=(batch, nt),
        in_specs=[
            pl.BlockSpec((tm, D), lambda b, i: (b * nt + i, 0)),
            _const_spec((D, 3 * D)),
            _const_spec((CONV_TAPS, D)),
            _const_spec((D, D)),
            _const_spec((1, D)),
            _const_spec((1, D)),
        ],
        out_specs=[
            pl.BlockSpec((tm, D), lambda b, i: (b * nt + i, 0)),
            pl.BlockSpec((1, CONV_TAPS - 1, D), lambda b, i: (b, 0, 0)),
        ],
        out_shape=[
            jax.ShapeDtypeStruct((batch * seq, D), F32),
            jax.ShapeDtypeStruct((batch, CONV_TAPS - 1, D), F32),
        ],
        scratch_shapes=[pltpu.VMEM((tm + SUBLANES, D), F32)],
        compiler_params=_params("arbitrary", "arbitrary"),
        name="conv_mixer_prompt",
    )(x, w_in, cw, w_out, g_pre, g_post)


def _conv_mixer_sample_kernel(x_ref, st0_ref, st1_ref, win_ref, cw_ref, wout_ref, gpre_ref,
                              gpost_ref, h_ref, st_ref, *, n, t):
    x = x_ref[...]
    a = _rms(x, gpre_ref[...]).astype(BF16)
    bcv = _dot(a, win_ref[...])
    u = bcv[:, D:2 * D] * bcv[:, 2 * D:]
    slabs = [st0_ref[...], st1_ref[...]] + [u[s * n:(s + 1) * n, :] for s in range(t)]
    cw = cw_ref[...]
    y = jnp.concatenate(
        [cw[0:1, :] * slabs[s] + cw[1:2, :] * slabs[s + 1] + cw[2:3, :] * slabs[s + 2]
         for s in range(t)], axis=0)
    out = _dot((bcv[:, :D] * y).astype(BF16), wout_ref[...])
    h_ref[...] = x + _rms(out, gpost_ref[...])
    st_ref[...] = jnp.concatenate(slabs[t:], axis=0)


def _conv_mixer_sample(x, st0, st1, w_in, cw, w_out, g_pre, g_post, n, t):
    rows = n * t
    shapes = [(rows, D), (n, D), (n, D), (D, 3 * D), (CONV_TAPS, D), (D, D), (1, D), (1, D)]
    return pl.pallas_call(
        functools.partial(_conv_mixer_sample_kernel, n=n, t=t),
        grid=(1,),
        in_specs=[_const_spec(s) for s in shapes],
        out_specs=[_const_spec((rows, D)), _const_spec(((CONV_TAPS - 1) * n, D))],
        out_shape=[
            jax.ShapeDtypeStruct((rows, D), F32),
            jax.ShapeDtypeStruct(((CONV_TAPS - 1) * n, D), F32),
        ],
        compiler_params=_params("arbitrary"),
        name="conv_mixer_sample",
    )(x, st0, st1, w_in, cw, w_out, g_pre, g_post)


def _mlp_tail(h, w1_ref, w2_ref, gpre_ref, gpost_ref, n_chunks):
    a = _rms(h, gpre_ref[...]).astype(BF16)
    ck = DFF // n_chunks
    acc = None
    for c in range(n_chunks):
        hm = jnp.maximum(_dot(a, w1_ref[:, c * ck:(c + 1) * ck]), 0.0)
        part = _dot((hm * hm).astype(BF16), w2_ref[c * ck:(c + 1) * ck, :])
        acc = part if acc is None else acc + part
    return h + _rms(acc, gpost_ref[...])


def _attn_out_mlp_kernel(o_ref, wo_ref, gmix_ref, h_ref, w1_ref, w2_ref, gpre_ref, gpost_ref, out_ref, *,
                         n_chunks):
    h = h_ref[...] + _rms(_dot(o_ref[...].astype(BF16), wo_ref[...]), gmix_ref[...])
    out_ref[...] = _mlp_tail(h, w1_ref, w2_ref, gpre_ref, gpost_ref, n_chunks)


def _attn_out_mlp(o, w_o, g_mix, h, w1, w2, g_pre, g_post, tm, name):
    rows = h.shape[0]
    row_spec = pl.BlockSpec((tm, D), lambda i: (i, 0))
    return pl.pallas_call(
        functools.partial(_attn_out_mlp_kernel, n_chunks=4),
        grid=(rows // tm,),
        in_specs=[row_spec, _const_spec((D, D)), _const_spec((1, D)), row_spec,
                  _const_spec((D, DFF)), _const_spec((DFF, D)), _const_spec((1, D)), _const_spec((1, D))],
        out_specs=row_spec,
        out_shape=jax.ShapeDtypeStruct((rows, D), F32),
        compiler_params=_params("arbitrary"),
        name=name,
    )(o, w_o, g_mix, h, w1, w2, g_pre, g_post)


def _mlp_kernel(h_ref, w1_ref, w2_ref, gpre_ref, gpost_ref, o_ref, *, n_chunks):
    o_ref[...] = _mlp_tail(h_ref[...], w1_ref, w2_ref, gpre_ref, gpost_ref, n_chunks)


def _mlp(h, w1, w2, g_pre, g_post, tm, name):
    rows = h.shape[0]
    return pl.pallas_call(
        functools.partial(_mlp_kernel, n_chunks=4),
        grid=(rows // tm,),
        in_specs=[
            pl.BlockSpec((tm, D), lambda i: (i, 0)),
            _const_spec((D, DFF)),
            _const_spec((DFF, D)),
            _const_spec((1, D)),
            _const_spec((1, D)),
        ],
        out_specs=pl.BlockSpec((tm, D), lambda i: (i, 0)),
        out_shape=jax.ShapeDtypeStruct((rows, D), F32),
        compiler_params=_params("arbitrary"),
        name=name,
    )(h, w1, w2, g_pre, g_post)


def _qkv_kernel(h_ref, wqkvt_ref, wft_ref, bft_ref, gpre_ref,
                qt_ref, kt_ref, vt_ref, krm_ref, vtb_ref, lft_ref, ft_ref, carry, *, tm, seq, q_scale):
    i = pl.program_id(1)
    a = _rms(h_ref[...], gpre_ref[...]).astype(BF16)
    qkvt = _dot_nt(wqkvt_ref[...], a)
    qt_ref[0] = (qkvt[:D, :] * q_scale).astype(BF16)
    kt = qkvt[D:2 * D, :]
    vt = qkvt[2 * D:, :]
    kt_ref[0] = kt
    vt_ref[0] = vt
    vtb_ref[0] = vt.astype(BF16)
    krm_ref[...] = kt.T.astype(BF16)

    lft = _log_sigmoid(_dot_nt(wft_ref[...], a) + bft_ref[...])
    lft_ref[0] = lft

    r = lax.broadcasted_iota(jnp.int32, (tm, tm), 0)
    c = lax.broadcasted_iota(jnp.int32, (tm, tm), 1)
    keep = r <= c
    if seq < tm:
        shift = seq.bit_length() - 1
        keep = keep & ((r >> shift) == (c >> shift))
    tri = jnp.where(keep, 1.0, 0.0).astype(BF16)
    hi, mid, lo = _split3(lft)
    f = _dot(hi, tri) + _dot(mid, tri) + _dot(lo, tri)
    if seq > tm:
        @pl.when(i == 0)
        def _():
            carry[...] = jnp.zeros((H, 1), F32)
        f = f + carry[...]
        carry[...] = f[:, tm - 1:tm]
    ft_ref[0] = f


def _qkv(h, w_qkvt, w_ft, b_ft, g_pre, batch, rows_per_batch, tm, seq, q_scale, name):
    assert seq & (seq - 1) == 0 and (seq == rows_per_batch if seq > tm else tm % seq == 0)
    nt = rows_per_batch // tm
    row_spec = pl.BlockSpec((tm, D), lambda b, i: (b * nt + i, 0))
    feat_spec = pl.BlockSpec((1, D, tm), lambda b, i: (b, 0, i))
    gate_spec = pl.BlockSpec((1, H, tm), lambda b, i: (b, 0, i))
    feat = lambda dt: jax.ShapeDtypeStruct((batch, D, rows_per_batch), dt)
    gate = jax.ShapeDtypeStruct((batch, H, rows_per_batch), F32)
    return pl.pallas_call(
        functools.partial(_qkv_kernel, tm=tm, seq=seq, q_scale=q_scale),
        grid=(batch, nt),
        in_specs=[row_spec, _const_spec((3 * D, D)), _const_spec((H, D)), _const_spec((H, 1)),
                  _const_spec((1, D))],
        out_specs=[feat_spec, feat_spec, feat_spec, row_spec, feat_spec, gate_spec, gate_spec],
        out_shape=[
            feat(BF16),
            feat(F32),
            feat(F32),
            jax.ShapeDtypeStruct((batch * rows_per_batch, D), BF16),
            feat(BF16),
            gate,
            gate,
        ],
        scratch_shapes=[pltpu.VMEM((H, 1), F32)],
        compiler_params=_params("arbitrary", "arbitrary"),
        name=name,
    )(h, w_qkvt, w_ft, b_ft, g_pre)


def _prompt_attn_keys(k_ref, ft_ref, kaug_ref, *, tq, seq):
    lane = lax.broadcasted_iota(jnp.int32, (tq, LANES), 1)

    def fill(c, carry):
        off = pl.multiple_of(c * tq, tq)
        kaug_ref[pl.ds(off, tq), 0:LANES] = k_ref[pl.ds(off, tq), :]
        aug = jnp.zeros((tq, LANES), F32)
        for hh in range(2):
            row = ft_ref[0, 0, hh:hh + 1, pl.ds(off, tq)] * (-LOG2E)
            col = jnp.broadcast_to(row, (LANES, tq)).T
            hi = col.astype(BF16).astype(F32)
            r1 = col - hi
            mid = r1.astype(BF16).astype(F32)
            lo = r1 - mid
            aug = jnp.where(lane == 3 * hh, hi,
                            jnp.where(lane == 3 * hh + 1, mid, jnp.where(lane == 3 * hh + 2, lo, aug)))
        kaug_ref[pl.ds(off, tq), LANES:2 * LANES] = aug.astype(BF16)
        return carry
    lax.fori_loop(0, seq // tq, fill, 0)


def _prompt_attn_tile(qi, qt_ref, vt_ref, o_ref, kaug_ref, acc_ref, s_ref, *, tq, before_loop, after_loop):
    qt = qt_ref[0]
    sub = lax.broadcasted_iota(jnp.int32, (2 * DH, tq), 0)
    zero = jnp.zeros_like(qt)
    q_heads = (jnp.where(sub < DH, qt, zero), jnp.where(sub >= DH, qt, zero))
    q_aug = []
    for hh in range(2):
        sel = jnp.where((sub >= 3 * hh) & (sub < 3 * hh + 3), 1.0, 0.0).astype(BF16)
        q_aug.append(jnp.concatenate([q_heads[hh], sel], axis=0))
    ones_rows = jnp.ones((2 * SUBLANES, tq), BF16)
    acc_ref[...] = jnp.zeros(acc_ref.shape, F32)
    neg = jnp.full((1, tq), -jnp.inf, F32)

    def scores(off):
        k = kaug_ref[pl.ds(off, tq), :]
        return tuple(_dot(k, q_aug[hh]) for hh in range(2))

    def consume(off, sts, stats, diagonal, before_head=(None, None)):
        vt = vt_ref[0, :, pl.ds(off, tq)]
        out = []
        for hh in range(2):
            if before_head[hh] is not None:
                before_head[hh]()
            m_prev = stats[hh]
            st = sts[hh]
            if diagonal:
                key = lax.broadcasted_iota(jnp.int32, (tq, tq), 0)
                qry = lax.broadcasted_iota(jnp.int32, (tq, tq), 1)
                st = jnp.where(key <= qry, st, -jnp.inf)
            m_new = jnp.maximum(m_prev, jnp.max(st, axis=0, keepdims=True))
            alpha = jnp.exp2(m_prev - m_new)
            p = jnp.exp2(st - m_new).astype(BF16)
            va = jnp.concatenate([vt[hh * DH:(hh + 1) * DH, :], ones_rows], axis=0)
            acc_ref[hh] = alpha * acc_ref[hh] + _dot(va, p)
            out.append(m_new)
        return tuple(out)

    def step(blk, stats, src, dst):
        sts = (s_ref[src, 0], s_ref[src, 1])
        nxt = scores(pl.multiple_of((blk + 1) * tq, tq))
        stats = consume(pl.multiple_of(blk * tq, tq), sts, stats, False)
        s_ref[dst, 0] = nxt[0]
        s_ref[dst, 1] = nxt[1]
        return stats

    def steps(first_blk, count, stats):
        for u in range(count):
            stats = step(first_blk + u, stats, u % 2, (u + 1) % 2)
        return stats

    first = scores(0)
    s_ref[0, 0] = first[0]
    s_ref[0, 1] = first[1]
    handoff = before_loop()
    stats = lax.fori_loop(0, qi // KV_UNROLL, lambda j, st: steps(KV_UNROLL * j, KV_UNROLL, st), (neg, neg))
    done = (qi // KV_UNROLL) * KV_UNROLL
    rem = qi - done
    stats = lax.cond(rem >= 2, lambda st: steps(done, 2, st), lambda st: st, stats)
    stats = lax.cond(rem % 2 == 1, lambda st: steps(qi - 1, 1, st), lambda st: st, stats)
    last = qi % 2
    consume(pl.multiple_of(qi * tq, tq), (s_ref[last, 0], s_ref[last, 1]), stats, True, after_loop(handoff))
    a0, a1 = acc_ref[0], acc_ref[1]
    ot = jnp.concatenate([a0[:DH] / a0[DH:DH + 1], a1[:DH] / a1[DH:DH + 1]], axis=0)
    o_ref[...] = ot.T.astype(BF16)


def _side_by_side(page_refs):
    return jnp.concatenate([r[...].reshape(D, PAGE).astype(BF16) for r in page_refs], axis=1)


class _SampleAttn:
    def __init__(self, q_ref, k2_ref, v2_ref, fnt_ref, o_ref, qbd, m_ref, l_ref, acc_ref, carry, t):
        self.q_ref, self.k2_ref, self.v2_ref, self.fnt_ref, self.o_ref = q_ref, k2_ref, v2_ref, fnt_ref, o_ref
        self.qbd, self.m_ref, self.l_ref, self.acc_ref, self.carry, self.t = qbd, m_ref, l_ref, acc_ref, carry, t
        self.rows = t * H
        row = lax.broadcasted_iota(jnp.int32, (self.rows, D), 0)
        lane = lax.broadcasted_iota(jnp.int32, (self.rows, D), 1)
        self.own_head = (lane >> 6) == (row & (H - 1))

    def init(self):
        q = self.q_ref[0]
        rep = jnp.concatenate([jnp.broadcast_to(q[s:s + 1, :], (H, D)) for s in range(self.t)], axis=0)
        self.qbd[...] = jnp.where(self.own_head, rep, 0.0).astype(BF16)
        self.m_ref[...] = jnp.full(self.m_ref.shape, -jnp.inf, F32)
        self.l_ref[...] = jnp.zeros(self.l_ref.shape, F32)
        self.acc_ref[...] = jnp.zeros(self.acc_ref.shape, F32)
        self.carry[...] = jnp.zeros(self.carry.shape, F32)

    def _update(self, s, pv_fn):
        m_prev = self.m_ref[...]
        m_new = jnp.maximum(m_prev, jnp.max(s, axis=1, keepdims=True))
        alpha = jnp.exp(m_prev - m_new)
        pr = jnp.exp(s - m_new).astype(BF16)
        self.l_ref[...] = alpha * self.l_ref[...] + jnp.sum(pr.astype(F32), axis=1, keepdims=True)
        self.acc_ref[...] = alpha * self.acc_ref[...] + pv_fn(pr)
        self.m_ref[...] = m_new

    def page_scores(self, cl_refs, ck_refs):
        lpt = jnp.concatenate([r[...] for r in cl_refs], axis=0)
        j = lax.broadcasted_iota(jnp.int32, (PAGE, PAGE), 0)
        kk = lax.broadcasted_iota(jnp.int32, (PAGE, PAGE), 1)
        later = jnp.where(j > kk, 1.0, 0.0).astype(BF16)
        hi, mid, lo = _split3(lpt)
        within = _dot(hi, later) + _dot(mid, later) + _dot(lo, later)
        after = self.carry[...]
        bias = []
        for i in range(len(cl_refs)):
            bias.append(jnp.concatenate([within[i * H:(i + 1) * H, :] + after] * self.t, axis=0))
            after = after + jnp.sum(lpt[i * H:(i + 1) * H, :], axis=1, keepdims=True)
        self.carry[...] = after
        return _dot(self.qbd[...], _side_by_side(ck_refs)) + jnp.concatenate(bias, axis=1)

    def page_values(self, s, cv_refs):
        half = len(cv_refs) // 2
        state = {}

        def first():
            m_prev = self.m_ref[...]
            m_new = jnp.maximum(m_prev, jnp.max(s, axis=1, keepdims=True))
            state["alpha"] = jnp.exp(m_prev - m_new)
            state["pr"] = pr = jnp.exp(s - m_new).astype(BF16)
            self.m_ref[...] = m_new
            state["pv"] = _dot_nt(pr[:, :half * PAGE], _side_by_side(cv_refs[:half]))

        def second():
            alpha, pr = state["alpha"], state["pr"]
            pv = state["pv"] + _dot_nt(pr[:, half * PAGE:], _side_by_side(cv_refs[half:]))
            self.l_ref[...] = alpha * self.l_ref[...] + jnp.sum(pr.astype(F32), axis=1, keepdims=True)
            self.acc_ref[...] = alpha * self.acc_ref[...] + pv

        return first, second

    def finish(self):
        t, rows = self.t, self.rows
        pad = jnp.zeros((PAGE - SUBLANES, D), F32)
        k2 = jnp.concatenate([self.k2_ref[0], pad], axis=0).astype(BF16)
        v2 = jnp.concatenate([self.v2_ref[0], pad], axis=0).astype(BF16)
        s2 = _dot_nt(self.qbd[...], k2) - jnp.concatenate([self.fnt_ref[0]] * t, axis=0)
        r2 = lax.broadcasted_iota(jnp.int32, (rows, PAGE), 0)
        c2 = lax.broadcasted_iota(jnp.int32, (rows, PAGE), 1)
        s2 = jnp.where(c2 <= (r2 >> 4), s2, -jnp.inf)
        self._update(s2, lambda pr: _dot(pr, v2))
        o = jnp.where(self.own_head, self.acc_ref[...] / self.l_ref[...], 0.0)
        self.o_ref[0] = jnp.sum(o.reshape(t, H, D), axis=1)


def _attn_kernel(pt_ref, qt_ref, k_ref, vt_ref, ft_ref, q8_ref, k8_ref, v8_ref, fnt_ref, *refs,
                 tq, seq, pg, t):
    del pt_ref
    cl_refs, ck_refs, cv_refs = refs[:pg], refs[pg:2 * pg], refs[2 * pg:3 * pg]
    op_ref, os_ref, kaug, acc_p, s_buf, qbd, m_s, l_s, acc_s, carry = refs[3 * pg:]
    qi = pl.program_id(2)
    sample = _SampleAttn(q8_ref, k8_ref, v8_ref, fnt_ref, os_ref, qbd, m_s, l_s, acc_s, carry, t)

    @pl.when(qi == 0)
    def _():
        sample.init()
        _prompt_attn_keys(k_ref, ft_ref, kaug, tq=tq, seq=seq)

    _prompt_attn_tile(qi, qt_ref, vt_ref, op_ref, kaug, acc_p, s_buf, tq=tq,
                      before_loop=lambda: sample.page_scores(cl_refs, ck_refs),
                      after_loop=lambda s: sample.page_values(s, cv_refs))

    @pl.when(qi == seq // tq - 1)
    def _():
        sample.finish()


def _attn(qt, k_rm, vtb, ft, batch, seq, page_table, q8, k8, v8, fnt, cache_kt, cache_vt, cache_lt, t,
          tq=512, pg=8):
    n, n_pages = page_table.shape
    nq = seq // tq
    hp = H // 2
    assert n == batch * hp and n_pages == nq * pg, (n, batch, n_pages, nq, pg)
    rows = t * H
    seq_of = lambda b, h: b * hp + h
    tok_spec = pl.BlockSpec((1, SUBLANES, D), lambda b, h, i, pt: (seq_of(b, h), 0, 0))

    def page(j, nd):
        return lambda b, h, i, pt: (0, pt[seq_of(b, h), n_pages - 1 - (i * pg + j)]) + (0,) * nd

    grid_spec = pltpu.PrefetchScalarGridSpec(
        num_scalar_prefetch=1,
        grid=(batch, hp, nq),
        in_specs=(
            [pl.BlockSpec((1, 2 * DH, tq), lambda b, h, i, pt: (b, h, i)),
             pl.BlockSpec((seq, 2 * DH), lambda b, h, i, pt: (b, h)),
             pl.BlockSpec((1, 2 * DH, seq), lambda b, h, i, pt: (b, h, 0)),
             pl.BlockSpec((1, 1, 2, seq), lambda b, h, i, pt: (b, h, 0, 0)),
             tok_spec, tok_spec, tok_spec,
             pl.BlockSpec((1, H, PAGE), lambda b, h, i, pt: (seq_of(b, h), 0, 0))]
            + [pl.BlockSpec((None, None, H, PAGE), page(j, 2)) for j in range(pg)]
            + [pl.BlockSpec((None, None, H, DH, PAGE), page(j, 3)) for j in range(pg)]
            + [pl.BlockSpec((None, None, H, DH, PAGE), page(j, 3)) for j in range(pg)]
        ),
        out_specs=[
            pl.BlockSpec((tq, 2 * DH), lambda b, h, i, pt: (b * nq + i, h)),
            pl.BlockSpec((1, t, D), lambda b, h, i, pt: (seq_of(b, h), 0, 0)),
        ],
        scratch_shapes=[
            pltpu.VMEM((seq, 4 * DH), BF16),
            pltpu.VMEM((2, DH + 2 * SUBLANES, tq), F32),
            pltpu.VMEM((2, 2, tq, tq), F32),
            pltpu.VMEM((rows, D), BF16),
            pltpu.VMEM((rows, 1), F32),
            pltpu.VMEM((rows, 1), F32),
            pltpu.VMEM((rows, D), F32),
            pltpu.VMEM((H, 1), F32),
        ],
    )
    return pl.pallas_call(
        functools.partial(_attn_kernel, tq=tq, seq=seq, pg=pg, t=t),
        grid_spec=grid_spec,
        out_shape=[jax.ShapeDtypeStruct((batch * seq, D), BF16), jax.ShapeDtypeStruct((n, t, D), F32)],
        compiler_params=_params("arbitrary", "arbitrary", "arbitrary"),
        name="attn",
    )(page_table, qt, k_rm, vtb, ft.reshape(batch, hp, 2, seq), q8, k8, v8, fnt,
      *([cache_lt] * pg), *([cache_kt] * pg), *([cache_vt] * pg))


def kernel(x_prompt, x_sample, state_conv, cache_k, cache_v, cache_logf, page_table, conv_w_in, conv_w, conv_w_out, attn_w_qkv, attn_w_f, attn_b_f, attn_w_o, mlp_w1, mlp_w2, norm_pre_mix, norm_post_mix, norm_pre_mlp, norm_post_mlp):
    batch, seq, _ = x_prompt.shape
    n, t, _ = x_sample.shape
    tm_p, tm_s = 512, n * t
    gain = lambda g, i: g[i].reshape(1, D)

    w_in, w_out = conv_w_in[0].astype(BF16), conv_w_out[0].astype(BF16)
    w1, w2 = mlp_w1[0].astype(BF16), mlp_w2[0].astype(BF16)
    hp, conv_p = _conv_mixer_prompt(x_prompt.reshape(batch * seq, D), w_in, conv_w[0], w_out,
                                    gain(norm_pre_mix, 0), gain(norm_post_mix, 0), batch, seq, tm_p)
    xs = x_sample.transpose(1, 0, 2).reshape(t * n, D)
    hs, conv_s = _conv_mixer_sample(xs, state_conv[0, :, 0], state_conv[0, :, 1], w_in, conv_w[0], w_out,
                                    gain(norm_pre_mix, 0), gain(norm_post_mix, 0), n, t)
    hp = _mlp(hp, w1, w2, gain(norm_pre_mlp, 0), gain(norm_post_mlp, 0), tm_p, "mlp0_prompt")
    hs = _mlp(hs, w1, w2, gain(norm_pre_mlp, 0), gain(norm_post_mlp, 0), tm_s, "mlp0_sample")
    hs = hs.reshape(t, n, D).transpose(1, 0, 2).reshape(n * t, D)

    w_qkvt, w_o = attn_w_qkv[0].T.astype(BF16), attn_w_o[0].astype(BF16)
    w_ft, b_ft = attn_w_f[0].T.astype(BF16), attn_b_f[0].reshape(H, 1)
    w1, w2 = mlp_w1[1].astype(BF16), mlp_w2[1].astype(BF16)

    qt, kt_p, vt_p, k_rm, vtb, lft_p, ft_p = _qkv(hp, w_qkvt, w_ft, b_ft, gain(norm_pre_mix, 1),
                                                  batch, seq, tm_p, seq, SCALE * LOG2E, "qkv_prompt")
    qt_s, kt_s, vt_s, _, _, lft_s, ft_s = _qkv(hs, w_qkvt, w_ft, b_ft, gain(norm_pre_mix, 1),
                                               1, n * t, tm_s, t, SCALE, "qkv_sample")
    k_s, v_s = kt_s[0].T, vt_s[0].T
    pad_rows = lambda a: jnp.pad(a.reshape(n, t, D), ((0, 0), (0, SUBLANES - t), (0, 0)))
    fnt = jnp.pad(ft_s[0].reshape(H, n, t).transpose(1, 0, 2), ((0, 0), (0, 0), (0, PAGE - t)))
    o_p, o_s = _attn(qt, k_rm, vtb, ft_p, batch, seq,
                     page_table, pad_rows(qt_s[0].T.astype(F32)), pad_rows(k_s), pad_rows(v_s), fnt,
                     cache_k.transpose(0, 1, 3, 4, 2), cache_v.transpose(0, 1, 3, 4, 2),
                     cache_logf.transpose(0, 1, 3, 2), t)
    hp = _attn_out_mlp(o_p, w_o, gain(norm_post_mix, 1), hp, w1, w2,
                       gain(norm_pre_mlp, 1), gain(norm_post_mlp, 1), tm_p, "attn_out_mlp1_prompt")
    hs = _attn_out_mlp(o_s.reshape(n * t, D), w_o, gain(norm_post_mix, 1), hs, w1, w2,
                       gain(norm_pre_mlp, 1), gain(norm_post_mlp, 1), tm_s, "attn_out_mlp1_sample")

    heads_t = lambda a: a.reshape(1, batch, H, DH, seq).transpose(0, 1, 4, 2, 3)
    return (hp.reshape(batch, seq, D), hs.reshape(n, t, D),
            conv_p[None], conv_s.reshape(CONV_TAPS - 1, n, D).transpose(1, 0, 2)[None],
            heads_t(kt_p), heads_t(vt_p), lft_p.transpose(0, 2, 1)[None],
            k_s.reshape(1, n, t, H, DH), v_s.reshape(1, n, t, H, DH), lft_s[0].T.reshape(1, n, t, H))
```

```python
import functools

import jax
import jax.numpy as jnp
from jax import lax
from jax.experimental import pallas as pl
from jax.experimental.pallas import tpu as pltpu

D = 1024
H = 16
DH = 64
DFF = 4096
CONV_TAPS = 3
EPS = 1e-6
SCALE = 0.125
PAGE = 128
LOG2E = 1.4426950408889634
KV_UNROLL = 4

F32 = jnp.float32
BF16 = jnp.bfloat16
NT = (((1,), (1,)), ((), ()))

VMEM_LIMIT_BYTES = 56 * 1024 * 1024
SUBLANES = 8
LANES = 128


def _params(*sem):
    return pltpu.CompilerParams(dimension_semantics=sem, vmem_limit_bytes=VMEM_LIMIT_BYTES)


def _const_spec(shape):
    nd = len(shape)
    return pl.BlockSpec(shape, lambda *_: (0,) * nd)


def _rms(x, g):
    r = lax.rsqrt(jnp.mean(x * x, axis=-1, keepdims=True) + EPS)
    return x * r * g


def _log_sigmoid(z):
    return jnp.minimum(z, 0.0) - jnp.log1p(jnp.exp(-jnp.abs(z)))


def _split3(x):
    hi = x.astype(BF16)
    r = x - hi.astype(F32)
    mid = r.astype(BF16)
    lo = (r - mid.astype(F32)).astype(BF16)
    return hi, mid, lo


def _dot(a, b):
    return jnp.dot(a, b, preferred_element_type=F32)


def _dot_nt(a, b):
    return lax.dot_general(a, b, NT, preferred_element_type=F32)


def _conv_mixer_prompt_kernel(x_ref, win_ref, cw_ref, wout_ref, gpre_ref, gpost_ref,
                              h_ref, st_ref, ubuf, *, tm, nt):
    i = pl.program_id(1)

    @pl.when(i == 0)
    def _():
        ubuf[0:SUBLANES, :] = jnp.zeros((SUBLANES, D), F32)

    x = x_ref[...]
    a = _rms(x, gpre_ref[...]).astype(BF16)
    bcv = _dot(a, win_ref[...])
    u = bcv[:, D:2 * D] * bcv[:, 2 * D:]
    ubuf[SUBLANES:SUBLANES + tm, :] = u
    cw = cw_ref[...]
    y = (cw[0:1, :] * ubuf[SUBLANES - 2:SUBLANES - 2 + tm, :]
         + cw[1:2, :] * ubuf[SUBLANES - 1:SUBLANES - 1 + tm, :]
         + cw[2:3, :] * u)
    out = _dot((bcv[:, :D] * y).astype(BF16), wout_ref[...])
    h_ref[...] = x + _rms(out, gpost_ref[...])
    ubuf[0:SUBLANES, :] = ubuf[tm:tm + SUBLANES, :]

    @pl.when(i == nt - 1)
    def _():
        st_ref[0] = ubuf[SUBLANES - 2:SUBLANES, :]


def _conv_mixer_prompt(x, w_in, cw, w_out, g_pre, g_post, batch, seq, tm=512):
    nt = seq // tm
    return pl.pallas_call(
        functools.partial(_conv_mixer_prompt_kernel, tm=tm, nt=nt),
        grid=(batch, nt),
        in_specs=[
            pl.BlockSpec((tm, D), lambda b, i: (b * nt + i, 0)),
            _const_spec((D, 3 * D)),
            _const_spec((CONV_TAPS, D)),
            _const_spec((D, D)),
            _const_spec((1, D)),
            _const_spec((1, D)),
        ],
        out_specs=[
            pl.BlockSpec((tm, D), lambda b, i: (b * nt + i, 0)),
            pl.BlockSpec((1, CONV_TAPS - 1, D), lambda b, i: (b, 0, 0)),
        ],
        out_shape=[
            jax.ShapeDtypeStruct((batch * seq, D), F32),
            jax.ShapeDtypeStruct((batch, CONV_TAPS - 1, D), F32),
        ],
        scratch_shapes=[pltpu.VMEM((tm + SUBLANES, D), F32)],
        compiler_params=_params("arbitrary", "arbitrary"),
        name="conv_mixer_prompt",
    )(x, w_in, cw, w_out, g_pre, g_post)


def _conv_mixer_sample_kernel(x_ref, st0_ref, st1_ref, win_ref, cw_ref, wout_ref, gpre_ref,
                              gpost_ref, h_ref, st_ref, *, n, t):
    x = x_ref[...]
    a = _rms(x, gpre_ref[...]).astype(BF16)
    bcv = _dot(a, win_ref[...])
    u = bcv[:, D:2 * D] * bcv[:, 2 * D:]
    slabs = [st0_ref[...], st1_ref[...]] + [u[s * n:(s + 1) * n, :] for s in range(t)]
    cw = cw_ref[...]
    y = jnp.concatenate(
        [cw[0:1, :] * slabs[s] + cw[1:2, :] * slabs[s + 1] + cw[2:3, :] * slabs[s + 2]
         for s in range(t)], axis=0)
    out = _dot((bcv[:, :D] * y).astype(BF16), wout_ref[...])
    h_ref[...] = x + _rms(out, gpost_ref[...])
    st_ref[...] = jnp.concatenate(slabs[t:], axis=0)


def _conv_mixer_sample(x, st0, st1, w_in, cw, w_out, g_pre, g_post, n, t):
    rows = n * t
    shapes = [(rows, D), (n, D), (n, D), (D, 3 * D), (CONV_TAPS, D), (D, D), (1, D), (1, D)]
    return pl.pallas_call(
        functools.partial(_conv_mixer_sample_kernel, n=n, t=t),
        grid=(1,),
        in_specs=[_const_spec(s) for s in shapes],
        out_specs=[_const_spec((rows, D)), _const_spec(((CONV_TAPS - 1) * n, D))],
        out_shape=[
            jax.ShapeDtypeStruct((rows, D), F32),
            jax.ShapeDtypeStruct(((CONV_TAPS - 1) * n, D), F32),
        ],
        compiler_params=_params("arbitrary"),
        name="conv_mixer_sample",
    )(x, st0, st1, w_in, cw, w_out, g_pre, g_post)


def _mlp_tail(h, w1_ref, w2_ref, gpre_ref, gpost_ref, n_chunks):
    a = _rms(h, gpre_ref[...]).astype(BF16)
    ck = DFF // n_chunks
    acc = None
    for c in range(n_chunks):
        hm = jnp.maximum(_dot(a, w1_ref[:, c * ck:(c + 1) * ck]), 0.0)
        part = _dot((hm * hm).astype(BF16), w2_ref[c * ck:(c + 1) * ck, :])
        acc = part if acc is None else acc + part
    return h + _rms(acc, gpost_ref[...])


def _attn_out_mlp_kernel(o_ref, wo_ref, gmix_ref, h_ref, w1_ref, w2_ref, gpre_ref, gpost_ref, out_ref, *,
                         n_chunks):
    h = h_ref[...] + _rms(_dot(o_ref[...].astype(BF16), wo_ref[...]), gmix_ref[...])
    out_ref[...] = _mlp_tail(h, w1_ref, w2_ref, gpre_ref, gpost_ref, n_chunks)


def _attn_out_mlp(o, w_o, g_mix, h, w1, w2, g_pre, g_post, tm, name):
    rows = h.shape[0]
    row_spec = pl.BlockSpec((tm, D), lambda i: (i, 0))
    return pl.pallas_call(
        functools.partial(_attn_out_mlp_kernel, n_chunks=4),
        grid=(rows // tm,),
        in_specs=[row_spec, _const_spec((D, D)), _const_spec((1, D)), row_spec,
                  _const_spec((D, DFF)), _const_spec((DFF, D)), _const_spec((1, D)), _const_spec((1, D))],
        out_specs=row_spec,
        out_shape=jax.ShapeDtypeStruct((rows, D), F32),
        compiler_params=_params("arbitrary"),
        name=name,
    )(o, w_o, g_mix, h, w1, w2, g_pre, g_post)


def _mlp_kernel(h_ref, w1_ref, w2_ref, gpre_ref, gpost_ref, o_ref, *, n_chunks):
    o_ref[...] = _mlp_tail(h_ref[...], w1_ref, w2_ref, gpre_ref, gpost_ref, n_chunks)


def _mlp(h, w1, w2, g_pre, g_post, tm, name):
    rows = h.shape[0]
    return pl.pallas_call(
        functools.partial(_mlp_kernel, n_chunks=4),
        grid=(rows // tm,),
        in_specs=[
            pl.BlockSpec((tm, D), lambda i: (i, 0)),
            _const_spec((D, DFF)),
            _const_spec((DFF, D)),
            _const_spec((1, D)),
            _const_spec((1, D)),
        ],
        out_specs=pl.BlockSpec((tm, D), lambda i: (i, 0)),
        out_shape=jax.ShapeDtypeStruct((rows, D), F32),
        compiler_params=_params("arbitrary"),
        name=name,
    )(h, w1, w2, g_pre, g_post)


def _qkv_kernel(h_ref, wqkvt_ref, wft_ref, bft_ref, gpre_ref,
                qt_ref, kt_ref, vt_ref, krm_ref, vtb_ref, lft_ref, ft_ref, carry, *, tm, seq, q_scale):
    i = pl.program_id(1)
    a = _rms(h_ref[...], gpre_ref[...]).astype(BF16)
    qkvt = _dot_nt(wqkvt_ref[...], a)
    qt_ref[0] = (qkvt[:D, :] * q_scale).astype(BF16)
    kt = qkvt[D:2 * D, :]
    vt = qkvt[2 * D:, :]
    kt_ref[0] = kt
    vt_ref[0] = vt
    vtb_ref[0] = vt.astype(BF16)
    krm_ref[...] = kt.T.astype(BF16)

    lft = _log_sigmoid(_dot_nt(wft_ref[...], a) + bft_ref[...])
    lft_ref[0] = lft

    r = lax.broadcasted_iota(jnp.int32, (tm, tm), 0)
    c = lax.broadcasted_iota(jnp.int32, (tm, tm), 1)
    keep = r <= c
    if seq < tm:
        shift = seq.bit_length() - 1
        keep = keep & ((r >> shift) == (c >> shift))
    tri = jnp.where(keep, 1.0, 0.0).astype(BF16)
    hi, mid, lo = _split3(lft)
    f = _dot(hi, tri) + _dot(mid, tri) + _dot(lo, tri)
    if seq > tm:
        @pl.when(i == 0)
        def _():
            carry[...] = jnp.zeros((H, 1), F32)
        f = f + carry[...]
        carry[...] = f[:, tm - 1:tm]
    ft_ref[0] = f


def _qkv(h, w_qkvt, w_ft, b_ft, g_pre, batch, rows_per_batch, tm, seq, q_scale, name):
    assert seq & (seq - 1) == 0 and (seq == rows_per_batch if seq > tm else tm % seq == 0)
    nt = rows_per_batch // tm
    row_spec = pl.BlockSpec((tm, D), lambda b, i: (b * nt + i, 0))
    feat_spec = pl.BlockSpec((1, D, tm), lambda b, i: (b, 0, i))
    gate_spec = pl.BlockSpec((1, H, tm), lambda b, i: (b, 0, i))
    feat = lambda dt: jax.ShapeDtypeStruct((batch, D, rows_per_batch), dt)
    gate = jax.ShapeDtypeStruct((batch, H, rows_per_batch), F32)
    return pl.pallas_call(
        functools.partial(_qkv_kernel, tm=tm, seq=seq, q_scale=q_scale),
        grid=(batch, nt),
        in_specs=[row_spec, _const_spec((3 * D, D)), _const_spec((H, D)), _const_spec((H, 1)),
                  _const_spec((1, D))],
        out_specs=[feat_spec, feat_spec, feat_spec, row_spec, feat_spec, gate_spec, gate_spec],
        out_shape=[
            feat(BF16),
            feat(F32),
            feat(F32),
            jax.ShapeDtypeStruct((batch * rows_per_batch, D), BF16),
            feat(BF16),
            gate,
            gate,
        ],
        scratch_shapes=[pltpu.VMEM((H, 1), F32)],
        compiler_params=_params("arbitrary", "arbitrary"),
        name=name,
    )(h, w_qkvt, w_ft, b_ft, g_pre)


def _prompt_attn_keys(k_ref, ft_ref, kaug_ref, *, tq, seq):
    lane = lax.broadcasted_iota(jnp.int32, (tq, LANES), 1)

    def fill(c, carry):
        off = pl.multiple_of(c * tq, tq)
        kaug_ref[pl.ds(off, tq), 0:LANES] = k_ref[pl.ds(off, tq), :]
        aug = jnp.zeros((tq, LANES), F32)
        for hh in range(2):
            row = ft_ref[0, 0, hh:hh + 1, pl.ds(off, tq)] * (-LOG2E)
            col = jnp.broadcast_to(row, (LANES, tq)).T
            hi = col.astype(BF16).astype(F32)
            r1 = col - hi
            mid = r1.astype(BF16).astype(F32)
            lo = r1 - mid
            aug = jnp.where(lane == 3 * hh, hi,
                            jnp.where(lane == 3 * hh + 1, mid, jnp.where(lane == 3 * hh + 2, lo, aug)))
        kaug_ref[pl.ds(off, tq), LANES:2 * LANES] = aug.astype(BF16)
        return carry
    lax.fori_loop(0, seq // tq, fill, 0)


def _prompt_attn_tile(qi, qt_ref, vt_ref, o_ref, kaug_ref, acc_ref, s_ref, *, tq, before_loop, after_loop):
    qt = qt_ref[0]
    sub = lax.broadcasted_iota(jnp.int32, (2 * DH, tq), 0)
    zero = jnp.zeros_like(qt)
    q_heads = (jnp.where(sub < DH, qt, zero), jnp.where(sub >= DH, qt, zero))
    q_aug = []
    for hh in range(2):
        sel = jnp.where((sub >= 3 * hh) & (sub < 3 * hh + 3), 1.0, 0.0).astype(BF16)
        q_aug.append(jnp.concatenate([q_heads[hh], sel], axis=0))
    ones_rows = jnp.ones((2 * SUBLANES, tq), BF16)
    acc_ref[...] = jnp.zeros(acc_ref.shape, F32)
    neg = jnp.full((1, tq), -jnp.inf, F32)

    def scores(off):
        k = kaug_ref[pl.ds(off, tq), :]
        return tuple(_dot(k, q_aug[hh]) for hh in range(2))

    def consume(off, sts, stats, diagonal, before_head=(None, None)):
        vt = vt_ref[0, :, pl.ds(off, tq)]
        out = []
        for hh in range(2):
            if before_head[hh] is not None:
                before_head[hh]()
            m_prev = stats[hh]
            st = sts[hh]
            if diagonal:
                key = lax.broadcasted_iota(jnp.int32, (tq, tq), 0)
                qry = lax.broadcasted_iota(jnp.int32, (tq, tq), 1)
                st = jnp.where(key <= qry, st, -jnp.inf)
            m_new = jnp.maximum(m_prev, jnp.max(st, axis=0, keepdims=True))
            alpha = jnp.exp2(m_prev - m_new)
            p = jnp.exp2(st - m_new).astype(BF16)
            va = jnp.concatenate([vt[hh * DH:(hh + 1) * DH, :], ones_rows], axis=0)
            acc_ref[hh] = alpha * acc_ref[hh] + _dot(va, p)
            out.append(m_new)
        return tuple(out)

    def step(blk, stats, src, dst):
        sts = (s_ref[src, 0], s_ref[src, 1])
        nxt = scores(pl.multiple_of((blk + 1) * tq, tq))
        stats = consume(pl.multiple_of(blk * tq, tq), sts, stats, False)
        s_ref[dst, 0] = nxt[0]
        s_ref[dst, 1] = nxt[1]
        return stats

    def steps(first_blk, count, stats):
        for u in range(count):
            stats = step(first_blk + u, stats, u % 2, (u + 1) % 2)
        return stats

    first = scores(0)
    s_ref[0, 0] = first[0]
    s_ref[0, 1] = first[1]
    handoff = before_loop()
    stats = lax.fori_loop(0, qi // KV_UNROLL, lambda j, st: steps(KV_UNROLL * j, KV_UNROLL, st), (neg, neg))
    done = (qi // KV_UNROLL) * KV_UNROLL
    rem = qi - done
    stats = lax.cond(rem >= 2, lambda st: steps(done, 2, st), lambda st: st, stats)
    stats = lax.cond(rem % 2 == 1, lambda st: steps(qi - 1, 1, st), lambda st: st, stats)
    last = qi % 2
    consume(pl.multiple_of(qi * tq, tq), (s_ref[last, 0], s_ref[last, 1]), stats, True, after_loop(handoff))
    a0, a1 = acc_ref[0], acc_ref[1]
    ot = jnp.concatenate([a0[:DH] / a0[DH:DH + 1], a1[:DH] / a1[DH:DH + 1]], axis=0)
    o_ref[...] = ot.T.astype(BF16)


def _side_by_side(page_refs):
    return jnp.concatenate([r[...].reshape(D, PAGE).astype(BF16) for r in page_refs], axis=1)


class _SampleAttn:
    def __init__(self, q_ref, k2_ref, v2_ref, fnt_ref, o_ref, qbd, m_ref, l_ref, acc_ref, carry, t):
        self.q_ref, self.k2_ref, self.v2_ref, self.fnt_ref, self.o_ref = q_ref, k2_ref, v2_ref, fnt_ref, o_ref
        self.qbd, self.m_ref, self.l_ref, self.acc_ref, self.carry, self.t = qbd, m_ref, l_ref, acc_ref, carry, t
        self.rows = t * H
        row = lax.broadcasted_iota(jnp.int32, (self.rows, D), 0)
        lane = lax.broadcasted_iota(jnp.int32, (self.rows, D), 1)
        self.own_head = (lane >> 6) == (row & (H - 1))

    def init(self):
        q = self.q_ref[0]
        rep = jnp.concatenate([jnp.broadcast_to(q[s:s + 1, :], (H, D)) for s in range(self.t)], axis=0)
        self.qbd[...] = jnp.where(self.own_head, rep, 0.0).astype(BF16)
        self.m_ref[...] = jnp.full(self.m_ref.shape, -jnp.inf, F32)
        self.l_ref[...] = jnp.zeros(self.l_ref.shape, F32)
        self.acc_ref[...] = jnp.zeros(self.acc_ref.shape, F32)
        self.carry[...] = jnp.zeros(self.carry.shape, F32)

    def _update(self, s, pv_fn):
        m_prev = self.m_ref[...]
        m_new = jnp.maximum(m_prev, jnp.max(s, axis=1, keepdims=True))
        alpha = jnp.exp(m_prev - m_new)
        pr = jnp.exp(s - m_new).astype(BF16)
        self.l_ref[...] = alpha * self.l_ref[...] + jnp.sum(pr.astype(F32), axis=1, keepdims=True)
        self.acc_ref[...] = alpha * self.acc_ref[...] + pv_fn(pr)
        self.m_ref[...] = m_new

    def page_scores(self, cl_refs, ck_refs):
        lpt = jnp.concatenate([r[...] for r in cl_refs], axis=0)
        j = lax.broadcasted_iota(jnp.int32, (PAGE, PAGE), 0)
        kk = lax.broadcasted_iota(jnp.int32, (PAGE, PAGE), 1)
        later = jnp.where(j > kk, 1.0, 0.0).astype(BF16)
        hi, mid, lo = _split3(lpt)
        within = _dot(hi, later) + _dot(mid, later) + _dot(lo, later)
        after = self.carry[...]
        bias = []
        for i in range(len(cl_refs)):
            bias.append(jnp.concatenate([within[i * H:(i + 1) * H, :] + after] * self.t, axis=0))
            after = after + jnp.sum(lpt[i * H:(i + 1) * H, :], axis=1, keepdims=True)
        self.carry[...] = after
        return _dot(self.qbd[...], _side_by_side(ck_refs)) + jnp.concatenate(bias, axis=1)

    def page_values(self, s, cv_refs):
        half = len(cv_refs) // 2
        state = {}

        def first():
            m_prev = self.m_ref[...]
            m_new = jnp.maximum(m_prev, jnp.max(s, axis=1, keepdims=True))
            state["alpha"] = jnp.exp(m_prev - m_new)
            state["pr"] = pr = jnp.exp(s - m_new).astype(BF16)
            self.m_ref[...] = m_new
            state["pv"] = _dot_nt(pr[:, :half * PAGE], _side_by_side(cv_refs[:half]))

        def second():
            alpha, pr = state["alpha"], state["pr"]
            pv = state["pv"] + _dot_nt(pr[:, half * PAGE:], _side_by_side(cv_refs[half:]))
            self.l_ref[...] = alpha * self.l_ref[...] + jnp.sum(pr.astype(F32), axis=1, keepdims=True)
            self.acc_ref[...] = alpha * self.acc_ref[...] + pv

        return first, second

    def finish(self):
        t, rows = self.t, self.rows
        pad = jnp.zeros((PAGE - SUBLANES, D), F32)
        k2 = jnp.concatenate([self.k2_ref[0], pad], axis=0).astype(BF16)
        v2 = jnp.concatenate([self.v2_ref[0], pad], axis=0).astype(BF16)
        s2 = _dot_nt(self.qbd[...], k2) - jnp.concatenate([self.fnt_ref[0]] * t, axis=0)
        r2 = lax.broadcasted_iota(jnp.int32, (rows, PAGE), 0)
        c2 = lax.broadcasted_iota(jnp.int32, (rows, PAGE), 1)
        s2 = jnp.where(c2 <= (r2 >> 4), s2, -jnp.inf)
        self._update(s2, lambda pr: _dot(pr, v2))
        o = jnp.where(self.own_head, self.acc_ref[...] / self.l_ref[...], 0.0)
        self.o_ref[0] = jnp.sum(o.reshape(t, H, D), axis=1)


def _attn_kernel(pt_ref, qt_ref, k_ref, vt_ref, ft_ref, q8_ref, k8_ref, v8_ref, fnt_ref, *refs,
                 tq, seq, pg, t):
    del pt_ref
    cl_refs, ck_refs, cv_refs = refs[:pg], refs[pg:2 * pg], refs[2 * pg:3 * pg]
    op_ref, os_ref, kaug, acc_p, s_buf, qbd, m_s, l_s, acc_s, carry = refs[3 * pg:]
    qi = pl.program_id(2)
    sample = _SampleAttn(q8_ref, k8_ref, v8_ref, fnt_ref, os_ref, qbd, m_s, l_s, acc_s, carry, t)

    @pl.when(qi == 0)
    def _():
        sample.init()
        _prompt_attn_keys(k_ref, ft_ref, kaug, tq=tq, seq=seq)

    _prompt_attn_tile(qi, qt_ref, vt_ref, op_ref, kaug, acc_p, s_buf, tq=tq,
                      before_loop=lambda: sample.page_scores(cl_refs, ck_refs),
                      after_loop=lambda s: sample.page_values(s, cv_refs))

    @pl.when(qi == seq // tq - 1)
    def _():
        sample.finish()


def _attn(qt, k_rm, vtb, ft, batch, seq, page_table, q8, k8, v8, fnt, cache_kt, cache_vt, cache_lt, t,
          tq=512, pg=8):
    n, n_pages = page_table.shape
    nq = seq // tq
    hp = H // 2
    assert n == batch * hp and n_pages == nq * pg, (n, batch, n_pages, nq, pg)
    rows = t * H
    seq_of = lambda b, h: b * hp + h
    tok_spec = pl.BlockSpec((1, SUBLANES, D), lambda b, h, i, pt: (seq_of(b, h), 0, 0))

    def page(j, nd):
        return lambda b, h, i, pt: (0, pt[seq_of(b, h), n_pages - 1 - (i * pg + j)]) + (0,) * nd

    grid_spec = pltpu.PrefetchScalarGridSpec(
        num_scalar_prefetch=1,
        grid=(batch, hp, nq),
        in_specs=(
            [pl.BlockSpec((1, 2 * DH, tq), lambda b, h, i, pt: (b, h, i)),
             pl.BlockSpec((seq, 2 * DH), lambda b, h, i, pt: (b, h)),
             pl.BlockSpec((1, 2 * DH, seq), lambda b, h, i, pt: (b, h, 0)),
             pl.BlockSpec((1, 1, 2, seq), lambda b, h, i, pt: (b, h, 0, 0)),
             tok_spec, tok_spec, tok_spec,
             pl.BlockSpec((1, H, PAGE), lambda b, h, i, pt: (seq_of(b, h), 0, 0))]
            + [pl.BlockSpec((None, None, H, PAGE), page(j, 2)) for j in range(pg)]
            + [pl.BlockSpec((None, None, H, DH, PAGE), page(j, 3)) for j in range(pg)]
            + [pl.BlockSpec((None, None, H, DH, PAGE), page(j, 3)) for j in range(pg)]
        ),
        out_specs=[
            pl.BlockSpec((tq, 2 * DH), lambda b, h, i, pt: (b * nq + i, h)),
            pl.BlockSpec((1, t, D), lambda b, h, i, pt: (seq_of(b, h), 0, 0)),
        ],
        scratch_shapes=[
            pltpu.VMEM((seq, 4 * DH), BF16),
            pltpu.VMEM((2, DH + 2 * SUBLANES, tq), F32),
            pltpu.VMEM((2, 2, tq, tq), F32),
            pltpu.VMEM((rows, D), BF16),
            pltpu.VMEM((rows, 1), F32),
            pltpu.VMEM((rows, 1), F32),
            pltpu.VMEM((rows, D), F32),
            pltpu.VMEM((H, 1), F32),
        ],
    )
    return pl.pallas_call(
        functools.partial(_attn_kernel, tq=tq, seq=seq, pg=pg, t=t),
        grid_spec=grid_spec,
        out_shape=[jax.ShapeDtypeStruct((batch * seq, D), BF16), jax.ShapeDtypeStruct((n, t, D), F32)],
        compiler_params=_params("arbitrary", "arbitrary", "arbitrary"),
        name="attn",
    )(page_table, qt, k_rm, vtb, ft.reshape(batch, hp, 2, seq), q8, k8, v8, fnt,
      *([cache_lt] * pg), *([cache_kt] * pg), *([cache_vt] * pg))


def kernel(x_prompt, x_sample, state_conv, cache_k, cache_v, cache_logf, page_table, conv_w_in, conv_w, conv_w_out, attn_w_qkv, attn_w_f, attn_b_f, attn_w_o, mlp_w1, mlp_w2, norm_pre_mix, norm_post_mix, norm_pre_mlp, norm_post_mlp):
    batch, seq, _ = x_prompt.shape
    n, t, _ = x_sample.shape
    tm_p, tm_s = 1024, n * t
    gain = lambda g, i: g[i].reshape(1, D)

    w_in, w_out = conv_w_in[0].astype(BF16), conv_w_out[0].astype(BF16)
    w1, w2 = mlp_w1[0].astype(BF16), mlp_w2[0].astype(BF16)
    hp, conv_p = _conv_mixer_prompt(x_prompt.reshape(batch * seq, D), w_in, conv_w[0], w_out,
                                    gain(norm_pre_mix, 0), gain(norm_post_mix, 0), batch, seq, tm_p)
    xs = x_sample.transpose(1, 0, 2).reshape(t * n, D)
    hs, conv_s = _conv_mixer_sample(xs, state_conv[0, :, 0], state_conv[0, :, 1], w_in, conv_w[0], w_out,
                                    gain(norm_pre_mix, 0), gain(norm_post_mix, 0), n, t)
    hp = _mlp(hp, w1, w2, gain(norm_pre_mlp, 0), gain(norm_post_mlp, 0), tm_p, "mlp0_prompt")
    hs = _mlp(hs, w1, w2, gain(norm_pre_mlp, 0), gain(norm_post_mlp, 0), tm_s, "mlp0_sample")
    hs = hs.reshape(t, n, D).transpose(1, 0, 2).reshape(n * t, D)

    w_qkvt, w_o = attn_w_qkv[0].T.astype(BF16), attn_w_o[0].astype(BF16)
    w_ft, b_ft = attn_w_f[0].T.astype(BF16), attn_b_f[0].reshape(H, 1)
    w1, w2 = mlp_w1[1].astype(BF16), mlp_w2[1].astype(BF16)

    qt, kt_p, vt_p, k_rm, vtb, lft_p, ft_p = _qkv(hp, w_qkvt, w_ft, b_ft, gain(norm_pre_mix, 1),
                                                  batch, seq, tm_p, seq, SCALE * LOG2E, "qkv_prompt")
    qt_s, kt_s, vt_s, _, _, lft_s, ft_s = _qkv(hs, w_qkvt, w_ft, b_ft, gain(norm_pre_mix, 1),
                                               1, n * t, tm_s, t, SCALE, "qkv_sample")
    k_s, v_s = kt_s[0].T, vt_s[0].T
    pad_rows = lambda a: jnp.pad(a.reshape(n, t, D), ((0, 0), (0, SUBLANES - t), (0, 0)))
    fnt = jnp.pad(ft_s[0].reshape(H, n, t).transpose(1, 0, 2), ((0, 0), (0, 0), (0, PAGE - t)))
    o_p, o_s = _attn(qt, k_rm, vtb, ft_p, batch, seq,
                     page_table, pad_rows(qt_s[0].T.astype(F32)), pad_rows(k_s), pad_rows(v_s), fnt,
                     cache_k.transpose(0, 1, 3, 4, 2), cache_v.transpose(0, 1, 3, 4, 2),
                     cache_logf.transpose(0, 1, 3, 2), t)
    hp = _attn_out_mlp(o_p, w_o, gain(norm_post_mix, 1), hp, w1, w2,
                       gain(norm_pre_mlp, 1), gain(norm_post_mlp, 1), tm_p, "attn_out_mlp1_prompt")
    hs = _attn_out_mlp(o_s.reshape(n * t, D), w_o, gain(norm_post_mix, 1), hs, w1, w2,
                       gain(norm_pre_mlp, 1), gain(norm_post_mlp, 1), tm_s, "attn_out_mlp1_sample")

    heads_t = lambda a: a.reshape(1, batch, H, DH, seq).transpose(0, 1, 4, 2, 3)
    return (hp.reshape(batch, seq, D), hs.reshape(n, t, D),
            conv_p[None], conv_s.reshape(CONV_TAPS - 1, n, D).transpose(1, 0, 2)[None],
            heads_t(kt_p), heads_t(vt_p), lft_p.transpose(0, 2, 1)[None],
            k_s.reshape(1, n, t, H, DH), v_s.reshape(1, n, t, H, DH), lft_s[0].T.reshape(1, n, t, H))
```

```python
import functools

import jax
import jax.numpy as jnp
from jax import lax
from jax.experimental import pallas as pl
from jax.experimental.pallas import tpu as pltpu

D = 1024
H = 16
DH = 64
DFF = 4096
CONV_TAPS = 3
EPS = 1e-6
SCALE = 0.125
PAGE = 128
LOG2E = 1.4426950408889634
KV_UNROLL = 8

F32 = jnp.float32
BF16 = jnp.bfloat16
NT = (((1,), (1,)), ((), ()))

VMEM_LIMIT_BYTES = 56 * 1024 * 1024
SUBLANES = 8
LANES = 128


def _params(*sem):
    return pltpu.CompilerParams(dimension_semantics=sem, vmem_limit_bytes=VMEM_LIMIT_BYTES)


def _const_spec(shape):
    nd = len(shape)
    return pl.BlockSpec(shape, lambda *_: (0,) * nd)


def _rms(x, g):
    r = lax.rsqrt(jnp.mean(x * x, axis=-1, keepdims=True) + EPS)
    return x * r * g


def _log_sigmoid(z):
    return jnp.minimum(z, 0.0) - jnp.log1p(jnp.exp(-jnp.abs(z)))


def _split3(x):
    hi = x.astype(BF16)
    r = x - hi.astype(F32)
    mid = r.astype(BF16)
    lo = (r - mid.astype(F32)).astype(BF16)
    return hi, mid, lo


def _dot(a, b):
    return jnp.dot(a, b, preferred_element_type=F32)


def _dot_nt(a, b):
    return lax.dot_general(a, b, NT, preferred_element_type=F32)


def _conv_mixer_prompt_kernel(x_ref, win_ref, cw_ref, wout_ref, gpre_ref, gpost_ref,
                              h_ref, st_ref, ubuf, *, tm, nt):
    i = pl.program_id(1)

    @pl.when(i == 0)
    def _():
        ubuf[0:SUBLANES, :] = jnp.zeros((SUBLANES, D), F32)

    x = x_ref[...]
    a = _rms(x, gpre_ref[...]).astype(BF16)
    bcv = _dot(a, win_ref[...])
    u = bcv[:, D:2 * D] * bcv[:, 2 * D:]
    ubuf[SUBLANES:SUBLANES + tm, :] = u
    cw = cw_ref[...]
    y = (cw[0:1, :] * ubuf[SUBLANES - 2:SUBLANES - 2 + tm, :]
         + cw[1:2, :] * ubuf[SUBLANES - 1:SUBLANES - 1 + tm, :]
         + cw[2:3, :] * u)
    out = _dot((bcv[:, :D] * y).astype(BF16), wout_ref[...])
    h_ref[...] = x + _rms(out, gpost_ref[...])
    ubuf[0:SUBLANES, :] = ubuf[tm:tm + SUBLANES, :]

    @pl.when(i == nt - 1)
    def _():
        st_ref[0] = ubuf[SUBLANES - 2:SUBLANES, :]


def _conv_mixer_prompt(x, w_in, cw, w_out, g_pre, g_post, batch, seq, tm=512):
    nt = seq // tm
    return pl.pallas_call(
        functools.partial(_conv_mixer_prompt_kernel, tm=tm, nt=nt),
        grid=(batch, nt),
        in_specs=[
            pl.BlockSpec((tm, D), lambda b, i: (b * nt + i, 0)),
            _const_spec((D, 3 * D)),
            _const_spec((CONV_TAPS, D)),
            _const_spec((D, D)),
            _const_spec((1, D)),
            _const_spec((1, D)),
        ],
        out_specs=[
            pl.BlockSpec((tm, D), lambda b, i: (b * nt + i, 0)),
            pl.BlockSpec((1, CONV_TAPS - 1, D), lambda b, i: (b, 0, 0)),
        ],
        out_shape=[
            jax.ShapeDtypeStruct((batch * seq, D), F32),
            jax.ShapeDtypeStruct((batch, CONV_TAPS - 1, D), F32),
        ],
        scratch_shapes=[pltpu.VMEM((tm + SUBLANES, D), F32)],
        compiler_params=_params("arbitrary", "arbitrary"),
        name="conv_mixer_prompt",
    )(x, w_in, cw, w_out, g_pre, g_post)


def _conv_mixer_sample_kernel(x_ref, st0_ref, st1_ref, win_ref, cw_ref, wout_ref, gpre_ref,
                              gpost_ref, h_ref, st_ref, *, n, t):
    x = x_ref[...]
    a = _rms(x, gpre_ref[...]).astype(BF16)
    bcv = _dot(a, win_ref[...])
    u = bcv[:, D:2 * D] * bcv[:, 2 * D:]
    slabs = [st0_ref[...], st1_ref[...]] + [u[s * n:(s + 1) * n, :] for s in range(t)]
    cw = cw_ref[...]
    y = jnp.concatenate(
        [cw[0:1, :] * slabs[s] + cw[1:2, :] * slabs[s + 1] + cw[2:3, :] * slabs[s + 2]
         for s in range(t)], axis=0)
    out = _dot((bcv[:, :D] * y).astype(BF16), wout_ref[...])
    h_ref[...] = x + _rms(out, gpost_ref[...])
    st_ref[...] = jnp.concatenate(slabs[t:], axis=0)


def _conv_mixer_sample(x, st0, st1, w_in, cw, w_out, g_pre, g_post, n, t):
    rows = n * t
    shapes = [(rows, D), (n, D), (n, D), (D, 3 * D), (CONV_TAPS, D), (D, D), (1, D), (1, D)]
    return pl.pallas_call(
        functools.partial(_conv_mixer_sample_kernel, n=n, t=t),
        grid=(1,),
        in_specs=[_const_spec(s) for s in shapes],
        out_specs=[_const_spec((rows, D)), _const_spec(((CONV_TAPS - 1) * n, D))],
        out_shape=[
            jax.ShapeDtypeStruct((rows, D), F32),
            jax.ShapeDtypeStruct(((CONV_TAPS - 1) * n, D), F32),
        ],
        compiler_params=_params("arbitrary"),
        name="conv_mixer_sample",
    )(x, st0, st1, w_in, cw, w_out, g_pre, g_post)


def _mlp_tail(h, w1_ref, w2_ref, gpre_ref, gpost_ref, n_chunks):
    a = _rms(h, gpre_ref[...]).astype(BF16)
    ck = DFF // n_chunks
    acc = None
    for c in range(n_chunks):
        hm = jnp.maximum(_dot(a, w1_ref[:, c * ck:(c + 1) * ck]), 0.0)
        part = _dot((hm * hm).astype(BF16), w2_ref[c * ck:(c + 1) * ck, :])
        acc = part if acc is None else acc + part
    return h + _rms(acc, gpost_ref[...])


def _attn_out_mlp_kernel(o_ref, wo_ref, gmix_ref, h_ref, w1_ref, w2_ref, gpre_ref, gpost_ref, out_ref, *,
                         n_chunks):
    h = h_ref[...] + _rms(_dot(o_ref[...].astype(BF16), wo_ref[...]), gmix_ref[...])
    out_ref[...] = _mlp_tail(h, w1_ref, w2_ref, gpre_ref, gpost_ref, n_chunks)


def _attn_out_mlp(o, w_o, g_mix, h, w1, w2, g_pre, g_post, tm, name):
    rows = h.shape[0]
    row_spec = pl.BlockSpec((tm, D), lambda i: (i, 0))
    return pl.pallas_call(
        functools.partial(_attn_out_mlp_kernel, n_chunks=4),
        grid=(rows // tm,),
        in_specs=[row_spec, _const_spec((D, D)), _const_spec((1, D)), row_spec,
                  _const_spec((D, DFF)), _const_spec((DFF, D)), _const_spec((1, D)), _const_spec((1, D))],
        out_specs=row_spec,
        out_shape=jax.ShapeDtypeStruct((rows, D), F32),
        compiler_params=_params("arbitrary"),
        name=name,
    )(o, w_o, g_mix, h, w1, w2, g_pre, g_post)


def _mlp_kernel(h_ref, w1_ref, w2_ref, gpre_ref, gpost_ref, o_ref, *, n_chunks):
    o_ref[...] = _mlp_tail(h_ref[...], w1_ref, w2_ref, gpre_ref, gpost_ref, n_chunks)


def _mlp(h, w1, w2, g_pre, g_post, tm, name):
    rows = h.shape[0]
    return pl.pallas_call(
        functools.partial(_mlp_kernel, n_chunks=4),
        grid=(rows // tm,),
        in_specs=[
            pl.BlockSpec((tm, D), lambda i: (i, 0)),
            _const_spec((D, DFF)),
            _const_spec((DFF, D)),
            _const_spec((1, D)),
            _const_spec((1, D)),
        ],
        out_specs=pl.BlockSpec((tm, D), lambda i: (i, 0)),
        out_shape=jax.ShapeDtypeStruct((rows, D), F32),
        compiler_params=_params("arbitrary"),
        name=name,
    )(h, w1, w2, g_pre, g_post)


def _qkv_kernel(h_ref, wqkvt_ref, wft_ref, bft_ref, gpre_ref,
                qt_ref, kt_ref, vt_ref, krm_ref, vtb_ref, lft_ref, ft_ref, carry, *, tm, seq, q_scale):
    i = pl.program_id(1)
    a = _rms(h_ref[...], gpre_ref[...]).astype(BF16)
    qkvt = _dot_nt(wqkvt_ref[...], a)
    qt_ref[0] = (qkvt[:D, :] * q_scale).astype(BF16)
    kt = qkvt[D:2 * D, :]
    vt = qkvt[2 * D:, :]
    kt_ref[0] = kt
    vt_ref[0] = vt
    vtb_ref[0] = vt.astype(BF16)
    krm_ref[...] = kt.T.astype(BF16)

    lft = _log_sigmoid(_dot_nt(wft_ref[...], a) + bft_ref[...])
    lft_ref[0] = lft

    r = lax.broadcasted_iota(jnp.int32, (tm, tm), 0)
    c = lax.broadcasted_iota(jnp.int32, (tm, tm), 1)
    keep = r <= c
    if seq < tm:
        shift = seq.bit_length() - 1
        keep = keep & ((r >> shift) == (c >> shift))
    tri = jnp.where(keep, 1.0, 0.0).astype(BF16)
    hi, mid, lo = _split3(lft)
    f = _dot(hi, tri) + _dot(mid, tri) + _dot(lo, tri)
    if seq > tm:
        @pl.when(i == 0)
        def _():
            carry[...] = jnp.zeros((H, 1), F32)
        f = f + carry[...]
        carry[...] = f[:, tm - 1:tm]
    ft_ref[0] = f


def _qkv(h, w_qkvt, w_ft, b_ft, g_pre, batch, rows_per_batch, tm, seq, q_scale, name):
    assert seq & (seq - 1) == 0 and (seq == rows_per_batch if seq > tm else tm % seq == 0)
    nt = rows_per_batch // tm
    row_spec = pl.BlockSpec((tm, D), lambda b, i: (b * nt + i, 0))
    feat_spec = pl.BlockSpec((1, D, tm), lambda b, i: (b, 0, i))
    gate_spec = pl.BlockSpec((1, H, tm), lambda b, i: (b, 0, i))
    feat = lambda dt: jax.ShapeDtypeStruct((batch, D, rows_per_batch), dt)
    gate = jax.ShapeDtypeStruct((batch, H, rows_per_batch), F32)
    return pl.pallas_call(
        functools.partial(_qkv_kernel, tm=tm, seq=seq, q_scale=q_scale),
        grid=(batch, nt),
        in_specs=[row_spec, _const_spec((3 * D, D)), _const_spec((H, D)), _const_spec((H, 1)),
                  _const_spec((1, D))],
        out_specs=[feat_spec, feat_spec, feat_spec, row_spec, feat_spec, gate_spec, gate_spec],
        out_shape=[
            feat(BF16),
            feat(F32),
            feat(F32),
            jax.ShapeDtypeStruct((batch * rows_per_batch, D), BF16),
            feat(BF16),
            gate,
            gate,
        ],
        scratch_shapes=[pltpu.VMEM((H, 1), F32)],
        compiler_params=_params("arbitrary", "arbitrary"),
        name=name,
    )(h, w_qkvt, w_ft, b_ft, g_pre)


def _prompt_attn_keys(k_ref, ft_ref, kaug_ref, *, tq, seq):
    lane = lax.broadcasted_iota(jnp.int32, (tq, LANES), 1)

    def fill(c, carry):
        off = pl.multiple_of(c * tq, tq)
        kaug_ref[pl.ds(off, tq), 0:LANES] = k_ref[pl.ds(off, tq), :]
        aug = jnp.zeros((tq, LANES), F32)
        for hh in range(2):
            row = ft_ref[0, 0, hh:hh + 1, pl.ds(off, tq)] * (-LOG2E)
            col = jnp.broadcast_to(row, (LANES, tq)).T
            hi = col.astype(BF16).astype(F32)
            r1 = col - hi
            mid = r1.astype(BF16).astype(F32)
            lo = r1 - mid
            aug = jnp.where(lane == 3 * hh, hi,
                            jnp.where(lane == 3 * hh + 1, mid, jnp.where(lane == 3 * hh + 2, lo, aug)))
        kaug_ref[pl.ds(off, tq), LANES:2 * LANES] = aug.astype(BF16)
        return carry
    lax.fori_loop(0, seq // tq, fill, 0)


def _prompt_attn_tile(qi, qt_ref, vt_ref, o_ref, kaug_ref, acc_ref, s_ref, *, tq, before_loop, after_loop):
    qt = qt_ref[0]
    sub = lax.broadcasted_iota(jnp.int32, (2 * DH, tq), 0)
    zero = jnp.zeros_like(qt)
    q_heads = (jnp.where(sub < DH, qt, zero), jnp.where(sub >= DH, qt, zero))
    q_aug = []
    for hh in range(2):
        sel = jnp.where((sub >= 3 * hh) & (sub < 3 * hh + 3), 1.0, 0.0).astype(BF16)
        q_aug.append(jnp.concatenate([q_heads[hh], sel], axis=0))
    ones_rows = jnp.ones((2 * SUBLANES, tq), BF16)
    acc_ref[...] = jnp.zeros(acc_ref.shape, F32)
    neg = jnp.full((1, tq), -jnp.inf, F32)

    def scores(off):
        k = kaug_ref[pl.ds(off, tq), :]
        return tuple(_dot(k, q_aug[hh]) for hh in range(2))

    def consume(off, sts, stats, diagonal, before_head=(None, None)):
        vt = vt_ref[0, :, pl.ds(off, tq)]
        out = []
        for hh in range(2):
            if before_head[hh] is not None:
                before_head[hh]()
            m_prev = stats[hh]
            st = sts[hh]
            if diagonal:
                key = lax.broadcasted_iota(jnp.int32, (tq, tq), 0)
                qry = lax.broadcasted_iota(jnp.int32, (tq, tq), 1)
                st = jnp.where(key <= qry, st, -jnp.inf)
            m_new = jnp.maximum(m_prev, jnp.max(st, axis=0, keepdims=True))
            alpha = jnp.exp2(m_prev - m_new)
            p = jnp.exp2(st - m_new).astype(BF16)
            va = jnp.concatenate([vt[hh * DH:(hh + 1) * DH, :], ones_rows], axis=0)
            acc_ref[hh] = alpha * acc_ref[hh] + _dot(va, p)
            out.append(m_new)
        return tuple(out)

    def step(blk, stats, src, dst):
        sts = (s_ref[src, 0], s_ref[src, 1])
        nxt = scores(pl.multiple_of((blk + 1) * tq, tq))
        stats = consume(pl.multiple_of(blk * tq, tq), sts, stats, False)
        s_ref[dst, 0] = nxt[0]
        s_ref[dst, 1] = nxt[1]
        return stats

    def steps(first_blk, count, stats):
        for u in range(count):
            stats = step(first_blk + u, stats, u % 2, (u + 1) % 2)
        return stats

    first = scores(0)
    s_ref[0, 0] = first[0]
    s_ref[0, 1] = first[1]
    handoff = before_loop()
    stats = lax.fori_loop(0, qi // KV_UNROLL, lambda j, st: steps(KV_UNROLL * j, KV_UNROLL, st), (neg, neg))
    group = KV_UNROLL // 2
    while group >= 1:
        start = qi - qi % (2 * group)
        stats = lax.cond(qi % (2 * group) >= group, lambda st, a=start, g=group: steps(a, g, st),
                         lambda st: st, stats)
        group //= 2
    last = qi % 2
    emit_first, emit_second = after_loop(handoff)
    consume(pl.multiple_of(qi * tq, tq), (s_ref[last, 0], s_ref[last, 1]), stats, True, (None, emit_first))
    emit_second()
    a0, a1 = acc_ref[0], acc_ref[1]
    ot = jnp.concatenate([a0[:DH] / a0[DH:DH + 1], a1[:DH] / a1[DH:DH + 1]], axis=0)
    o_ref[...] = ot.T.astype(BF16)


def _side_by_side(page_refs):
    return jnp.concatenate([r[...].reshape(D, PAGE).astype(BF16) for r in page_refs], axis=1)


class _SampleAttn:
    def __init__(self, q_ref, k2_ref, v2_ref, fnt_ref, o_ref, qbd, m_ref, l_ref, acc_ref, carry, t):
        self.q_ref, self.k2_ref, self.v2_ref, self.fnt_ref, self.o_ref = q_ref, k2_ref, v2_ref, fnt_ref, o_ref
        self.qbd, self.m_ref, self.l_ref, self.acc_ref, self.carry, self.t = qbd, m_ref, l_ref, acc_ref, carry, t
        self.rows = t * H
        row = lax.broadcasted_iota(jnp.int32, (self.rows, D), 0)
        lane = lax.broadcasted_iota(jnp.int32, (self.rows, D), 1)
        self.own_head = (lane >> 6) == (row & (H - 1))

    def init(self):
        q = self.q_ref[0]
        rep = jnp.concatenate([jnp.broadcast_to(q[s:s + 1, :], (H, D)) for s in range(self.t)], axis=0)
        self.qbd[...] = jnp.where(self.own_head, rep, 0.0).astype(BF16)
        self.m_ref[...] = jnp.full(self.m_ref.shape, -jnp.inf, F32)
        self.l_ref[...] = jnp.zeros(self.l_ref.shape, F32)
        self.acc_ref[...] = jnp.zeros(self.acc_ref.shape, F32)
        self.carry[...] = jnp.zeros(self.carry.shape, F32)

    def _update(self, s, pv_fn):
        m_prev = self.m_ref[...]
        m_new = jnp.maximum(m_prev, jnp.max(s, axis=1, keepdims=True))
        alpha = jnp.exp(m_prev - m_new)
        pr = jnp.exp(s - m_new).astype(BF16)
        self.l_ref[...] = alpha * self.l_ref[...] + jnp.sum(pr.astype(F32), axis=1, keepdims=True)
        self.acc_ref[...] = alpha * self.acc_ref[...] + pv_fn(pr)
        self.m_ref[...] = m_new

    def page_scores(self, cl_refs, ck_refs):
        lpt = jnp.concatenate([r[...] for r in cl_refs], axis=0)
        j = lax.broadcasted_iota(jnp.int32, (PAGE, PAGE), 0)
        kk = lax.broadcasted_iota(jnp.int32, (PAGE, PAGE), 1)
        later = jnp.where(j > kk, 1.0, 0.0).astype(BF16)
        hi, mid, lo = _split3(lpt)
        within = _dot(hi, later) + _dot(mid, later) + _dot(lo, later)
        after = self.carry[...]
        bias = []
        for i in range(len(cl_refs)):
            bias.append(jnp.concatenate([within[i * H:(i + 1) * H, :] + after] * self.t, axis=0))
            after = after + jnp.sum(lpt[i * H:(i + 1) * H, :], axis=1, keepdims=True)
        self.carry[...] = after
        return _dot(self.qbd[...], _side_by_side(ck_refs)) + jnp.concatenate(bias, axis=1)

    def page_values(self, s, cv_refs):
        half = len(cv_refs) // 2
        state = {}

        def first():
            m_prev = self.m_ref[...]
            m_new = jnp.maximum(m_prev, jnp.max(s, axis=1, keepdims=True))
            state["alpha"] = jnp.exp(m_prev - m_new)
            state["pr"] = pr = jnp.exp(s - m_new).astype(BF16)
            self.m_ref[...] = m_new
            state["pv"] = _dot_nt(pr[:, :half * PAGE], _side_by_side(cv_refs[:half]))

        def second():
            alpha, pr = state["alpha"], state["pr"]
            pv = state["pv"] + _dot_nt(pr[:, half * PAGE:], _side_by_side(cv_refs[half:]))
            self.l_ref[...] = alpha * self.l_ref[...] + jnp.sum(pr.astype(F32), axis=1, keepdims=True)
            self.acc_ref[...] = alpha * self.acc_ref[...] + pv

        return first, second

    def finish(self):
        t, rows = self.t, self.rows
        pad = jnp.zeros((PAGE - SUBLANES, D), F32)
        k2 = jnp.concatenate([self.k2_ref[0], pad], axis=0).astype(BF16)
        v2 = jnp.concatenate([self.v2_ref[0], pad], axis=0).astype(BF16)
        s2 = _dot_nt(self.qbd[...], k2) - jnp.concatenate([self.fnt_ref[0]] * t, axis=0)
        r2 = lax.broadcasted_iota(jnp.int32, (rows, PAGE), 0)
        c2 = lax.broadcasted_iota(jnp.int32, (rows, PAGE), 1)
        s2 = jnp.where(c2 <= (r2 >> 4), s2, -jnp.inf)
        self._update(s2, lambda pr: _dot(pr, v2))
        o = jnp.where(self.own_head, self.acc_ref[...] / self.l_ref[...], 0.0)
        self.o_ref[0] = jnp.sum(o.reshape(t, H, D), axis=1)


def _attn_kernel(pt_ref, qt_ref, k_ref, vt_ref, ft_ref, q8_ref, k8_ref, v8_ref, fnt_ref, *refs,
                 tq, seq, pg, t):
    del pt_ref
    cl_refs, ck_refs, cv_refs = refs[:pg], refs[pg:2 * pg], refs[2 * pg:3 * pg]
    op_ref, os_ref, kaug, acc_p, s_buf, qbd, m_s, l_s, acc_s, carry = refs[3 * pg:]
    qi = pl.program_id(2)
    sample = _SampleAttn(q8_ref, k8_ref, v8_ref, fnt_ref, os_ref, qbd, m_s, l_s, acc_s, carry, t)

    @pl.when(qi == 0)
    def _():
        sample.init()
        _prompt_attn_keys(k_ref, ft_ref, kaug, tq=tq, seq=seq)

    _prompt_attn_tile(qi, qt_ref, vt_ref, op_ref, kaug, acc_p, s_buf, tq=tq,
                      before_loop=lambda: sample.page_scores(cl_refs, ck_refs),
                      after_loop=lambda s: sample.page_values(s, cv_refs))

    @pl.when(qi == seq // tq - 1)
    def _():
        sample.finish()


def _attn(qt, k_rm, vtb, ft, batch, seq, page_table, q8, k8, v8, fnt, cache_kt, cache_vt, cache_lt, t,
          tq=512, pg=8):
    n, n_pages = page_table.shape
    nq = seq // tq
    hp = H // 2
    assert n == batch * hp and n_pages == nq * pg, (n, batch, n_pages, nq, pg)
    rows = t * H
    seq_of = lambda b, h: b * hp + h
    tok_spec = pl.BlockSpec((1, SUBLANES, D), lambda b, h, i, pt: (seq_of(b, h), 0, 0))

    def page(j, nd):
        return lambda b, h, i, pt: (0, pt[seq_of(b, h), n_pages - 1 - (i * pg + j)]) + (0,) * nd

    grid_spec = pltpu.PrefetchScalarGridSpec(
        num_scalar_prefetch=1,
        grid=(batch, hp, nq),
        in_specs=(
            [pl.BlockSpec((1, 2 * DH, tq), lambda b, h, i, pt: (b, h, i)),
             pl.BlockSpec((seq, 2 * DH), lambda b, h, i, pt: (b, h)),
             pl.BlockSpec((1, 2 * DH, seq), lambda b, h, i, pt: (b, h, 0)),
             pl.BlockSpec((1, 1, 2, seq), lambda b, h, i, pt: (b, h, 0, 0)),
             tok_spec, tok_spec, tok_spec,
             pl.BlockSpec((1, H, PAGE), lambda b, h, i, pt: (seq_of(b, h), 0, 0))]
            + [pl.BlockSpec((None, None, H, PAGE), page(j, 2)) for j in range(pg)]
            + [pl.BlockSpec((None, None, H, DH, PAGE), page(j, 3)) for j in range(pg)]
            + [pl.BlockSpec((None, None, H, DH, PAGE), page(j, 3)) for j in range(pg)]
        ),
        out_specs=[
            pl.BlockSpec((tq, 2 * DH), lambda b, h, i, pt: (b * nq + i, h)),
            pl.BlockSpec((1, t, D), lambda b, h, i, pt: (seq_of(b, h), 0, 0)),
        ],
        scratch_shapes=[
            pltpu.VMEM((seq, 4 * DH), BF16),
            pltpu.VMEM((2, DH + 2 * SUBLANES, tq), F32),
            pltpu.VMEM((2, 2, tq, tq), F32),
            pltpu.VMEM((rows, D), BF16),
            pltpu.VMEM((rows, 1), F32),
            pltpu.VMEM((rows, 1), F32),
            pltpu.VMEM((rows, D), F32),
            pltpu.VMEM((H, 1), F32),
        ],
    )
    return pl.pallas_call(
        functools.partial(_attn_kernel, tq=tq, seq=seq, pg=pg, t=t),
        grid_spec=grid_spec,
        out_shape=[jax.ShapeDtypeStruct((batch * seq, D), BF16), jax.ShapeDtypeStruct((n, t, D), F32)],
        compiler_params=_params("arbitrary", "arbitrary", "arbitrary"),
        name="attn",
    )(page_table, qt, k_rm, vtb, ft.reshape(batch, hp, 2, seq), q8, k8, v8, fnt,
      *([cache_lt] * pg), *([cache_kt] * pg), *([cache_vt] * pg))


def kernel(x_prompt, x_sample, state_conv, cache_k, cache_v, cache_logf, page_table, conv_w_in, conv_w, conv_w_out, attn_w_qkv, attn_w_f, attn_b_f, attn_w_o, mlp_w1, mlp_w2, norm_pre_mix, norm_post_mix, norm_pre_mlp, norm_post_mlp):
    batch, seq, _ = x_prompt.shape
    n, t, _ = x_sample.shape
    tm_p, tm_s = 1024, n * t
    gain = lambda g, i: g[i].reshape(1, D)

    w_in, w_out = conv_w_in[0].astype(BF16), conv_w_out[0].astype(BF16)
    w1, w2 = mlp_w1[0].astype(BF16), mlp_w2[0].astype(BF16)
    hp, conv_p = _conv_mixer_prompt(x_prompt.reshape(batch * seq, D), w_in, conv_w[0], w_out,
                                    gain(norm_pre_mix, 0), gain(norm_post_mix, 0), batch, seq, tm_p)
    xs = x_sample.transpose(1, 0, 2).reshape(t * n, D)
    hs, conv_s = _conv_mixer_sample(xs, state_conv[0, :, 0], state_conv[0, :, 1], w_in, conv_w[0], w_out,
                                    gain(norm_pre_mix, 0), gain(norm_post_mix, 0), n, t)
    hp = _mlp(hp, w1, w2, gain(norm_pre_mlp, 0), gain(norm_post_mlp, 0), tm_p, "mlp0_prompt")
    hs = _mlp(hs, w1, w2, gain(norm_pre_mlp, 0), gain(norm_post_mlp, 0), tm_s, "mlp0_sample")
    hs = hs.reshape(t, n, D).transpose(1, 0, 2).reshape(n * t, D)

    w_qkvt, w_o = attn_w_qkv[0].T.astype(BF16), attn_w_o[0].astype(BF16)
    w_ft, b_ft = attn_w_f[0].T.astype(BF16), attn_b_f[0].reshape(H, 1)
    w1, w2 = mlp_w1[1].astype(BF16), mlp_w2[1].astype(BF16)

    qt, kt_p, vt_p, k_rm, vtb, lft_p, ft_p = _qkv(hp, w_qkvt, w_ft, b_ft, gain(norm_pre_mix, 1),
                                                  batch, seq, tm_p, seq, SCALE * LOG2E, "qkv_prompt")
    qt_s, kt_s, vt_s, _, _, lft_s, ft_s = _qkv(hs, w_qkvt, w_ft, b_ft, gain(norm_pre_mix, 1),
                                               1, n * t, tm_s, t, SCALE, "qkv_sample")
    k_s, v_s = kt_s[0].T, vt_s[0].T
    pad_rows = lambda a: jnp.pad(a.reshape(n, t, D), ((0, 0), (0, SUBLANES - t), (0, 0)))
    fnt = jnp.pad(ft_s[0].reshape(H, n, t).transpose(1, 0, 2), ((0, 0), (0, 0), (0, PAGE - t)))
    o_p, o_s = _attn(qt, k_rm, vtb, ft_p, batch, seq,
                     page_table, pad_rows(qt_s[0].T.astype(F32)), pad_rows(k_s), pad_rows(v_s), fnt,
                     cache_k.transpose(0, 1, 3, 4, 2), cache_v.transpose(0, 1, 3, 4, 2),
                     cache_logf.transpose(0, 1, 3, 2), t)
    hp = _attn_out_mlp(o_p, w_o, gain(norm_post_mix, 1), hp, w1, w2,
                       gain(norm_pre_mlp, 1), gain(norm_post_mlp, 1), tm_p, "attn_out_mlp1_prompt")
    hs = _attn_out_mlp(o_s.reshape(n * t, D), w_o, gain(norm_post_mix, 1), hs, w1, w2,
                       gain(norm_pre_mlp, 1), gain(norm_post_mlp, 1), tm_s, "attn_out_mlp1_sample")

    heads_t = lambda a: a.reshape(1, batch, H, DH, seq).transpose(0, 1, 4, 2, 3)
    return (hp.reshape(batch, seq, D), hs.reshape(n, t, D),
            conv_p[None], conv_s.reshape(CONV_TAPS - 1, n, D).transpose(1, 0, 2)[None],
            heads_t(kt_p), heads_t(vt_p), lft_p.transpose(0, 2, 1)[None],
            k_s.reshape(1, n, t, H, DH), v_s.reshape(1, n, t, H, DH), lft_s[0].T.reshape(1, n, t, H))
```
